```python
import math
import jax, jax.numpy as jnp
from jax import lax
import numpy as np

D_MODEL = 1024
BATCH = 1
SEQ = 16384
DEPTH = 2

N_EVEN = (DEPTH + 1) // 2
N_ODD = DEPTH // 2

LRU_WIDTH = 512
LRU_BLOCKS = 8
LRU_BLOCK = LRU_WIDTH // LRU_BLOCKS
LRU_CONV = 4
LRU_C = 8.0

MLA_HEADS = 8
Q_LORA = 256
KV_LORA = 128
QK_NOPE = 64
QK_ROPE = 32
V_HEAD = 64
ROPE_THETA = 10000.0
Q_BLOCK = 128

EVEN_IN = 2 * LRU_WIDTH + Q_LORA + KV_LORA + QK_ROPE
EVEN_MIX = LRU_WIDTH + MLA_HEADS * V_HEAD

M_HEADS = 4
M_QK = 128
M_V = 256
M_CHUNK = 64
GATE_CAP = 15.0
ODD_IN = 2 * M_HEADS * M_QK + 2 * M_HEADS * M_V + 2 * M_HEADS
ODD_MIX = M_HEADS * M_V

D_FF = 2816
FFN_CONV = 3
EPS = 1e-6

kernel_name = "hybrid_rglru_mla_mlstm_convffn"


def rms_norm(x, g):
    xf = x.astype(jnp.float32)
    y = xf * lax.rsqrt(jnp.mean(xf * xf, axis=-1, keepdims=True) + EPS)
    return (y * g.astype(jnp.float32)).astype(x.dtype)


def causal_dwconv(x, w, b):
    k = w.shape[0]
    s = x.shape[1]
    xp = jnp.pad(x, ((0, 0), (k - 1, 0), (0, 0)))
    y = b
    for j in range(k):
        y = y + xp[:, j:j + s] * w[j]
    return y


def rope_cos_sin(positions):
    half = QK_ROPE // 2
    inv_freq = ROPE_THETA ** (-jnp.arange(half, dtype=jnp.float32) / half)
    ang = positions.astype(jnp.float32)[..., None] * inv_freq
    return jnp.cos(ang), jnp.sin(ang)


def apply_rope(x, cos, sin):
    x1, x2 = jnp.split(x.astype(jnp.float32), 2, axis=-1)
    return jnp.concatenate([x1 * cos - x2 * sin, x2 * cos + x1 * sin], axis=-1).astype(x.dtype)


def rg_lru(xr, gate, conv_w, conv_b, w_a, b_a, w_x, b_x, lam):
    bsz, s, _ = xr.shape
    xc = causal_dwconv(xr, conv_w, conv_b).astype(jnp.float32)
    xb = xc.reshape(bsz, s, LRU_BLOCKS, LRU_BLOCK)
    r = jax.nn.sigmoid(jnp.einsum("bsgi,gij->bsgj", xb, w_a.astype(jnp.float32)) + b_a).reshape(bsz, s, LRU_WIDTH)
    i = jax.nn.sigmoid(jnp.einsum("bsgi,gij->bsgj", xb, w_x.astype(jnp.float32)) + b_x).reshape(bsz, s, LRU_WIDTH)
    log_a = -LRU_C * r * jax.nn.softplus(-lam.astype(jnp.float32))
    a = jnp.exp(log_a)
    u = jnp.sqrt(-jnp.expm1(2.0 * log_a)) * (i * xc)

    def combine(left, right):
        a1, b1 = left
        a2, b2 = right
        return a1 * a2, a2 * b1 + b2

    _, h = lax.associative_scan(combine, (a, u), axis=1)
    return (h * jax.nn.gelu(gate.astype(jnp.float32))).astype(xr.dtype)


def causal_attention(q, k, v):
    bsz, s, h, dq = q.shape
    nb = s // Q_BLOCK
    qb = jnp.moveaxis(q.reshape(bsz, nb, Q_BLOCK, h, dq), 1, 0)
    key_idx = jnp.arange(s)
    neg = jnp.finfo(jnp.float32).min

    def one_block(args):
        blk, qi = args
        sc = jnp.einsum("bqhd,bkhd->bhqk", qi, k).astype(jnp.float32)
        q_idx = blk * Q_BLOCK + jnp.arange(Q_BLOCK)
        sc = jnp.where(key_idx[None, :] <= q_idx[:, None], sc, neg)
        p = jax.nn.softmax(sc, axis=-1).astype(v.dtype)
        return jnp.einsum("bhqk,bkhd->bqhd", p, v)

    out = lax.map(one_block, (jnp.arange(nb), qb))
    return jnp.moveaxis(out, 0, 1).reshape(bsz, s, h, v.shape[-1])


def mla(q_lat, kv_lat, k_rope, positions, q_norm_g, w_qb, kv_norm_g, w_kvb):
    bsz, s, _ = q_lat.shape
    q = (rms_norm(q_lat, q_norm_g) @ w_qb).reshape(bsz, s, MLA_HEADS, QK_NOPE + QK_ROPE)
    kv = (rms_norm(kv_lat, kv_norm_g) @ w_kvb).reshape(bsz, s, MLA_HEADS, QK_NOPE + V_HEAD)
    k_nope, v = kv[..., :QK_NOPE], kv[..., QK_NOPE:]
    cos, sin = rope_cos_sin(positions)
    q_pe = apply_rope(q[..., QK_NOPE:], cos[:, :, None, :], sin[:, :, None, :])
    k_pe = apply_rope(k_rope, cos, sin)
    scale = (QK_NOPE + QK_ROPE) ** -0.5
    q_full = jnp.concatenate([q[..., :QK_NOPE], q_pe], axis=-1) * scale
    k_full = jnp.concatenate([k_nope, jnp.broadcast_to(k_pe[:, :, None, :], (bsz, s, MLA_HEADS, QK_ROPE))], axis=-1)
    out = causal_attention(q_full, k_full, v)
    return out.reshape(bsz, s, MLA_HEADS * V_HEAD)


def even_mixer(h, positions, w_in, lru_conv_w, lru_conv_b, lru_w_a, lru_b_a, lru_w_x, lru_b_x, lru_lambda,
               q_norm_g, w_qb, kv_norm_g, w_kvb, w_out):
    z = h @ w_in
    c1 = LRU_WIDTH
    c2 = 2 * LRU_WIDTH
    c3 = c2 + Q_LORA
    c4 = c3 + KV_LORA
    xr, gate, q_lat, kv_lat, k_rope = jnp.split(z, [c1, c2, c3, c4], axis=-1)
    y_lru = rg_lru(xr, gate, lru_conv_w, lru_conv_b, lru_w_a, lru_b_a, lru_w_x, lru_b_x, lru_lambda)
    y_mla = mla(q_lat, kv_lat, k_rope, positions, q_norm_g, w_qb, kv_norm_g, w_kvb)
    return jnp.concatenate([y_lru, y_mla.astype(y_lru.dtype)], axis=-1) @ w_out


def chunk_view(t, nc):
    bsz, s, h = t.shape[:3]
    t = t.reshape(bsz, nc, M_CHUNK, h, *t.shape[3:])
    return jnp.moveaxis(t, (1, 3), (0, 2))


def mlstm(q, k, v, i_pre, f_pre):
    bsz, s, h, dk = q.shape
    dv = v.shape[-1]
    nc = s // M_CHUNK
    q = q * dk ** -0.5
    log_f = jax.nn.log_sigmoid(f_pre)
    causal = jnp.tril(jnp.ones((M_CHUNK, M_CHUNK), dtype=bool))

    def step(carry, inp):
        c_st, n_st, m_st = carry
        qc, kc, vc, ic, fc = inp
        b = jnp.cumsum(fc, axis=-1)
        d = jnp.where(causal, b[..., :, None] - b[..., None, :] + ic[..., None, :], -jnp.inf)
        inter = b + m_st[..., None]
        m_t = jnp.maximum(inter, jnp.max(d, axis=-1))
        w = jnp.exp(d - m_t[..., None])
        g = jnp.exp(inter - m_t)
        sc = jnp.einsum("bhtd,bhsd->bhts", qc, kc) * w
        num = g[..., None] * jnp.einsum("bhtd,bhde->bhte", qc, c_st) + jnp.einsum("bhts,bhse->bhte", sc, vc)
        den = g * jnp.einsum("bhtd,bhd->bht", qc, n_st) + jnp.sum(sc, axis=-1)
        h_out = num / jnp.maximum(jnp.abs(den), jnp.exp(-m_t))[..., None]
        b_last = b[..., -1]
        w_end = b_last[..., None] - b + ic
        m_new = jnp.maximum(b_last + m_st, jnp.max(w_end, axis=-1))
        g_end = jnp.exp(b_last + m_st - m_new)
        w_s = jnp.exp(w_end - m_new[..., None])
        c_new = g_end[..., None, None] * c_st + jnp.einsum("bhs,bhsd,bhse->bhde", w_s, kc, vc)
        n_new = g_end[..., None] * n_st + jnp.einsum("bhs,bhsd->bhd", w_s, kc)
        return (c_new, n_new, m_new), h_out

    init = (jnp.zeros((bsz, h, dk, dv), jnp.float32), jnp.zeros((bsz, h, dk), jnp.float32),
            jnp.zeros((bsz, h), jnp.float32))
    xs = (chunk_view(q, nc), chunk_view(k, nc), chunk_view(v, nc), chunk_view(i_pre, nc), chunk_view(log_f, nc))
    _, hs = lax.scan(step, init, xs)
    return jnp.moveaxis(hs, (0, 2), (1, 3)).reshape(bsz, s, h, dv)


def soft_cap(t):
    return GATE_CAP * jnp.tanh(t / GATE_CAP)


def odd_mixer(h, w_in, b_igate, b_fgate, out_norm_g, w_out):
    bsz, s, _ = h.shape
    z = (h @ w_in).astype(jnp.float32)
    c1 = M_HEADS * M_QK
    c2 = 2 * c1
    c3 = c2 + M_HEADS * M_V
    c4 = c3 + M_HEADS * M_V
    c5 = c4 + M_HEADS
    q, k, v, o, ig, fg = jnp.split(z, [c1, c2, c3, c4, c5], axis=-1)
    ig = soft_cap(ig + b_igate.astype(jnp.float32))
    fg = soft_cap(fg + b_fgate.astype(jnp.float32))
    hh = mlstm(q.reshape(bsz, s, M_HEADS, M_QK), k.reshape(bsz, s, M_HEADS, M_QK),
               v.reshape(bsz, s, M_HEADS, M_V), ig, fg)
    hh = hh * lax.rsqrt(jnp.mean(hh * hh, axis=-1, keepdims=True) + EPS)
    y = hh.reshape(bsz, s, ODD_MIX) * out_norm_g.astype(jnp.float32) * jax.nn.sigmoid(o)
    return y.astype(h.dtype) @ w_out


def conv_ffn(h, w_up, conv_w, conv_b, w_down):
    u = causal_dwconv(h @ w_up, conv_w, conv_b)
    g, val = jnp.split(u, 2, axis=-1)
    return (jax.nn.silu(g) * val) @ w_down


def setup_inputs(seed: int = 0) -> dict:
    key = jax.random.key(seed)
    ks = iter(jax.random.split(key, 40))

    def w(shape, fan_in):
        return jax.random.normal(next(ks), shape, jnp.float32) * fan_in ** -0.5

    def gain(shape):
        return 1.0 + 0.05 * jax.random.normal(next(ks), shape, jnp.float32)

    def bias(shape, scale=0.02):
        return scale * jax.random.normal(next(ks), shape, jnp.float32)

    x = jax.random.normal(next(ks), (BATCH, SEQ, D_MODEL), jnp.float32)
    positions = jnp.broadcast_to(jnp.arange(SEQ, dtype=jnp.int32)[None, :], (BATCH, SEQ))
    a_init = jax.random.uniform(next(ks), (N_EVEN, LRU_WIDTH), jnp.float32, minval=0.9, maxval=0.999)
    lru_lambda = jnp.log(a_init) - jnp.log1p(-a_init)
    return {
        "x": x,
        "positions": positions,
        "e_norm_g": gain((N_EVEN, D_MODEL)),
        "e_w_in": w((N_EVEN, D_MODEL, EVEN_IN), D_MODEL),
        "e_lru_conv_w": w((N_EVEN, LRU_CONV, LRU_WIDTH), LRU_CONV),
        "e_lru_conv_b": bias((N_EVEN, LRU_WIDTH)),
        "e_lru_w_a": w((N_EVEN, LRU_BLOCKS, LRU_BLOCK, LRU_BLOCK), LRU_BLOCK),
        "e_lru_b_a": bias((N_EVEN, LRU_BLOCKS, LRU_BLOCK)),
        "e_lru_w_x": w((N_EVEN, LRU_BLOCKS, LRU_BLOCK, LRU_BLOCK), LRU_BLOCK),
        "e_lru_b_x": bias((N_EVEN, LRU_BLOCKS, LRU_BLOCK)),
        "e_lru_lambda": lru_lambda,
        "e_q_norm_g": gain((N_EVEN, Q_LORA)),
        "e_w_qb": w((N_EVEN, Q_LORA, MLA_HEADS * (QK_NOPE + QK_ROPE)), Q_LORA),
        "e_kv_norm_g": gain((N_EVEN, KV_LORA)),
        "e_w_kvb": w((N_EVEN, KV_LORA, MLA_HEADS * (QK_NOPE + V_HEAD)), KV_LORA),
        "e_w_out": w((N_EVEN, EVEN_MIX, D_MODEL), EVEN_MIX),
        "o_norm_g": gain((N_ODD, D_MODEL)),
        "o_w_in": w((N_ODD, D_MODEL, ODD_IN), D_MODEL),
        "o_b_igate": bias((N_ODD, M_HEADS), 0.1),
        "o_b_fgate": 3.0 + 3.0 * jax.random.uniform(next(ks), (N_ODD, M_HEADS), jnp.float32),
        "o_out_norm_g": gain((N_ODD, ODD_MIX)),
        "o_w_out": w((N_ODD, ODD_MIX, D_MODEL), ODD_MIX),
        "f_norm_g": gain((DEPTH, D_MODEL)),
        "f_w_up": w((DEPTH, D_MODEL, 2 * D_FF), D_MODEL),
        "f_conv_w": w((DEPTH, FFN_CONV, 2 * D_FF), FFN_CONV),
        "f_conv_b": bias((DEPTH, 2 * D_FF)),
        "f_w_down": w((DEPTH, D_FF, D_MODEL), D_FF),
        "final_norm_g": gain((D_MODEL,)),
    }


def reference(x, positions, e_norm_g, e_w_in, e_lru_conv_w, e_lru_conv_b, e_lru_w_a, e_lru_b_a, e_lru_w_x,
              e_lru_b_x, e_lru_lambda, e_q_norm_g, e_w_qb, e_kv_norm_g, e_w_kvb, e_w_out, o_norm_g, o_w_in,
              o_b_igate, o_b_fgate, o_out_norm_g, o_w_out, f_norm_g, f_w_up, f_conv_w, f_conv_b, f_w_down,
              final_norm_g):
    h = x
    for layer in range(DEPTH):
        j = layer // 2
        if layer % 2 == 0:
            h = h + even_mixer(rms_norm(h, e_norm_g[j]), positions, e_w_in[j], e_lru_conv_w[j], e_lru_conv_b[j],
                               e_lru_w_a[j], e_lru_b_a[j], e_lru_w_x[j], e_lru_b_x[j], e_lru_lambda[j],
                               e_q_norm_g[j], e_w_qb[j], e_kv_norm_g[j], e_w_kvb[j], e_w_out[j])
        else:
            h = h + odd_mixer(rms_norm(h, o_norm_g[j]), o_w_in[j], o_b_igate[j], o_b_fgate[j],
                              o_out_norm_g[j], o_w_out[j])
        h = h + conv_ffn(rms_norm(h, f_norm_g[layer]), f_w_up[layer], f_conv_w[layer], f_conv_b[layer],
                         f_w_down[layer])
    return rms_norm(h, final_norm_g)
```

```python
import functools

import jax
import jax.numpy as jnp
from jax import lax
from jax.experimental import pallas as pl
from jax.experimental.pallas import tpu as pltpu

F32 = jnp.float32
BF16 = jnp.bfloat16

EPS = 1e-6
LANES = 128
SUBLANES = 8
VMEM_LIMIT = 56 * 1024 * 1024

D_MODEL = 1024
LRU_WIDTH = 512
LRU_BLOCKS = 8
LRU_CONV = 4
LRU_C = 8.0
MLA_HEADS = 8
Q_LORA = 256
KV_LORA = 128
QK_NOPE = 64
QK_ROPE = 32
V_HEAD = 64
ROPE_THETA = 10000.0
HEAD_PAD = 128
M_HEADS = 4
M_QK = 128
M_V = 256
GATE_CAP = 15.0
D_FF = 2816
FFN_CONV = 3

EVEN_COLS = 2 * LRU_WIDTH + Q_LORA + KV_LORA + 2 * HEAD_PAD
ODD_GATE_COL = 2 * M_HEADS * M_QK + 2 * M_HEADS * M_V
ODD_COLS = ODD_GATE_COL + LANES

ROW_BLOCK = 512
ATTN_BLOCK = 512
M_CHUNK = 128
FFN_CHUNK = 256


def _params(*sem):
    return pltpu.CompilerParams(dimension_semantics=sem, vmem_limit_bytes=VMEM_LIMIT)


def _rms(x, g):
    inv = lax.rsqrt(jnp.mean(x * x, axis=-1, keepdims=True) + EPS)
    return x * inv * g


def _shift_rows(x, prev_tail, sh):
    rolled = pltpu.roll(x, sh, axis=0)
    row = lax.broadcasted_iota(jnp.int32, prev_tail.shape, 0)
    head = jnp.where(row < sh, pltpu.roll(prev_tail, sh, axis=0), rolled[:SUBLANES])
    return jnp.concatenate([head, rolled[SUBLANES:]], axis=0)


def _norm_matmul_body(x_ref, g_ref, w_ref, o_ref):
    xn = _rms(x_ref[...], g_ref[...]).astype(BF16)
    o_ref[...] = jnp.dot(xn, w_ref[...], preferred_element_type=F32)


def _norm_matmul(x, g, w, name):
    s, d = x.shape
    n = w.shape[1]
    tm = min(ROW_BLOCK, s)
    return pl.pallas_call(
        _norm_matmul_body,
        grid=(s // tm,),
        in_specs=[
            pl.BlockSpec((tm, d), lambda i: (i, 0)),
            pl.BlockSpec((1, d), lambda i: (0, 0)),
            pl.BlockSpec((d, n), lambda i: (0, 0)),
        ],
        out_specs=pl.BlockSpec((tm, n), lambda i: (i, 0)),
        out_shape=jax.ShapeDtypeStruct((s, n), F32),
        compiler_params=_params("parallel"),
        name=name,
    )(x, g.reshape(1, d), w)


def _rglru_body(xr_ref, gate_ref, cw_ref, cb_ref, wa_ref, ba_ref, wx_ref, bx_ref, lam_ref,
                o_ref, tail_s, h_s, a_s, b_s):
    @pl.when(pl.program_id(0) == 0)
    def _():
        tail_s[...] = jnp.zeros_like(tail_s)
        h_s[...] = jnp.zeros_like(h_s)

    tm = xr_ref.shape[0]
    xr = xr_ref[...]
    tail = tail_s[...]
    tail_s[...] = xr[tm - SUBLANES:, :]
    xc = cb_ref[...] + _shift_rows(xr, tail, 3) * cw_ref[0:1, :]
    xc = xc + _shift_rows(xr, tail, 2) * cw_ref[1:2, :]
    xc = xc + _shift_rows(xr, tail, 1) * cw_ref[2:3, :]
    xc = xc + xr * cw_ref[3:4, :]

    xcb = xc.astype(BF16)
    r = jax.nn.sigmoid(jnp.dot(xcb, wa_ref[...], preferred_element_type=F32) + ba_ref[...])
    i = jax.nn.sigmoid(jnp.dot(xcb, wx_ref[...], preferred_element_type=F32) + bx_ref[...])
    neg_lam = -lam_ref[...]
    softplus = jnp.maximum(neg_lam, 0.0) + jnp.log1p(jnp.exp(-jnp.abs(neg_lam)))
    log_a = (-LRU_C * r) * softplus
    a = jnp.exp(log_a)
    th = jnp.tanh(log_a)
    u = jnp.sqrt(-2.0 * th / (1.0 - th)) * (i * xc)

    sub = lax.broadcasted_iota(jnp.int32, a.shape, 0) & (SUBLANES - 1)
    for d in (1, 2, 4):
        keep = sub >= d
        a_sh = jnp.where(keep, pltpu.roll(a, d, axis=0), 1.0)
        u_sh = jnp.where(keep, pltpu.roll(u, d, axis=0), 0.0)
        u = a * u_sh + u
        a = a * a_sh
    a_s[...] = a
    b_s[...] = u

    def group(k, h):
        off = pl.multiple_of(k * SUBLANES, SUBLANES)
        ht = a_s[pl.ds(off, SUBLANES), :] * h + b_s[pl.ds(off, SUBLANES), :]
        b_s[pl.ds(off, SUBLANES), :] = ht
        return ht[SUBLANES - 1:, :]

    h_s[...] = lax.fori_loop(0, tm // SUBLANES, group, h_s[...], unroll=8)
    o_ref[...] = (b_s[...] * jax.nn.gelu(gate_ref[...])).astype(o_ref.dtype)


def _rglru(z, cw, cb, wa, ba, wx, bx, lam):
    s = z.shape[0]
    w = LRU_WIDTH
    tm = min(ROW_BLOCK, s)
    vec = lambda r: pl.BlockSpec((r, w), lambda i: (0, 0))
    return pl.pallas_call(
        _rglru_body,
        grid=(s // tm,),
        in_specs=[
            pl.BlockSpec((tm, w), lambda i: (i, 0)),
            pl.BlockSpec((tm, w), lambda i: (i, 1)),
            vec(LRU_CONV), vec(1), vec(w), vec(1), vec(w), vec(1), vec(1),
        ],
        out_specs=pl.BlockSpec((tm, w), lambda i: (i, 0)),
        out_shape=jax.ShapeDtypeStruct((s, w), BF16),
        scratch_shapes=[
            pltpu.VMEM((SUBLANES, w), F32),
            pltpu.VMEM((1, w), F32),
            pltpu.VMEM((tm, w), F32),
            pltpu.VMEM((tm, w), F32),
        ],
        compiler_params=_params("arbitrary"),
        name="rglru",
    )(z, z, cw, cb, wa, ba, wx, bx, lam)


def _mla_prep_body(zq_ref, zkv_ref, zkr_ref, zkrot_ref, pos_ref, qg_ref, kvg_ref,
                   wqa_ref, wqb_ref, wk_ref, wv_ref, freq_ref, q_ref, k_ref, v_ref):
    ang = pos_ref[...].astype(F32) * freq_ref[...]
    cosv = jnp.cos(ang)
    sinv = jnp.sin(ang)
    lane = lax.broadcasted_iota(jnp.int32, (1, HEAD_PAD), 1)
    scale = (QK_NOPE + QK_ROPE) ** -0.5
    cq = jnp.where(lane < QK_NOPE + QK_ROPE, cosv, 0.0) * scale
    sq = sinv * scale
    ck = jnp.where(lane >= QK_NOPE, cosv, 0.0)

    qn = _rms(zq_ref[...], qg_ref[...]).astype(BF16)
    qa = jnp.dot(qn, wqa_ref[...], preferred_element_type=F32)
    qb = jnp.dot(qn, wqb_ref[...], preferred_element_type=F32)
    kvn = _rms(zkv_ref[...], kvg_ref[...]).astype(BF16)
    kn = jnp.dot(kvn, wk_ref[...], preferred_element_type=F32)
    v_ref[...] = jnp.dot(kvn, wv_ref[...], preferred_element_type=F32).astype(BF16)
    kpe = zkr_ref[...] * ck + zkrot_ref[...] * sinv
    for h in range(MLA_HEADS):
        hs = slice(h * HEAD_PAD, (h + 1) * HEAD_PAD)
        q_ref[:, hs] = (qa[:, hs] * cq + qb[:, hs] * sq).astype(BF16)
        k_ref[:, hs] = (kn[:, hs] + kpe).astype(BF16)


def _mla_prep(z, pos, qg, kvg, wqa, wqb, wk, wv, freq):
    s = z.shape[0]
    tm = min(ROW_BLOCK, s)
    hw = MLA_HEADS * HEAD_PAD
    c0 = 2 * LRU_WIDTH
    full = lambda a: pl.BlockSpec(a.shape, lambda i: (0, 0))
    out = jax.ShapeDtypeStruct((s, hw), BF16)
    return pl.pallas_call(
        _mla_prep_body,
        grid=(s // tm,),
        in_specs=[
            pl.BlockSpec((tm, Q_LORA), lambda i: (i, c0 // Q_LORA)),
            pl.BlockSpec((tm, KV_LORA), lambda i: (i, (c0 + Q_LORA) // KV_LORA)),
            pl.BlockSpec((tm, HEAD_PAD), lambda i: (i, (c0 + Q_LORA + KV_LORA) // HEAD_PAD)),
            pl.BlockSpec((tm, HEAD_PAD), lambda i: (i, (c0 + Q_LORA + KV_LORA) // HEAD_PAD + 1)),
            pl.BlockSpec((tm, 1), lambda i: (i, 0)),
            full(qg), full(kvg), full(wqa), full(wqb), full(wk), full(wv), full(freq),
        ],
        out_specs=[pl.BlockSpec((tm, hw), lambda i: (i, 0))] * 3,
        out_shape=[out, out, out],
        compiler_params=_params("parallel"),
        name="mla_prep",
    )(z, z, z, z, pos, qg, kvg, wqa, wqb, wk, wv, freq)


def _attn_body(q_ref, k_ref, v_ref, o_ref):
    tq = q_ref.shape[0]
    tk = tq
    qi = pl.program_id(1)
    q = q_ref[...]

    def scores(j):
        off = pl.multiple_of(j * tk, tk)
        k = k_ref[pl.ds(off, tk), :]
        v = v_ref[pl.ds(off, tk), :]
        return lax.dot_general(q, k, (((1,), (1,)), ((), ())), preferred_element_type=F32), v

    def update(carry, s, v):
        m, l, acc = carry
        m_new = jnp.maximum(m, jnp.max(s, axis=-1, keepdims=True))
        p = jnp.exp(s - m_new)
        alpha = jnp.exp(m - m_new)
        l = alpha * l + jnp.sum(p, axis=-1, keepdims=True)
        acc = alpha * acc + jnp.dot(p.astype(BF16), v, preferred_element_type=F32)
        return m_new, l, acc

    def full_block(j, carry):
        s, v = scores(j)
        return update(carry, s, v)

    init = (jnp.full((tq, 1), -jnp.inf, F32), jnp.zeros((tq, 1), F32), jnp.zeros((tq, HEAD_PAD), F32))
    carry = lax.fori_loop(0, qi, full_block, init)
    s, v = scores(qi)
    row = lax.broadcasted_iota(jnp.int32, s.shape, 0)
    col = lax.broadcasted_iota(jnp.int32, s.shape, 1)
    s = jnp.where(col <= row, s, jnp.finfo(F32).min)
    _, l, acc = update(carry, s, v)
    o_ref[...] = (acc / l).astype(o_ref.dtype)


def _attention(q, k, v):
    s, hw = q.shape
    tq = min(ATTN_BLOCK, s)
    return pl.pallas_call(
        _attn_body,
        grid=(hw // HEAD_PAD, s // tq),
        in_specs=[
            pl.BlockSpec((tq, HEAD_PAD), lambda h, i: (i, h)),
            pl.BlockSpec((s, HEAD_PAD), lambda h, i: (0, h)),
            pl.BlockSpec((s, HEAD_PAD), lambda h, i: (0, h)),
        ],
        out_specs=pl.BlockSpec((tq, HEAD_PAD), lambda h, i: (i, h)),
        out_shape=jax.ShapeDtypeStruct((s, hw), BF16),
        compiler_params=_params("parallel", "parallel"),
        name="attention",
    )(q, k, v)


def _mix_out_body(*refs):
    res_ref, o_ref = refs[0], refs[-1]
    acc = res_ref[...]
    n_in = (len(refs) - 2) // 2
    for y_ref, w_ref in zip(refs[1:1 + n_in], refs[1 + n_in:1 + 2 * n_in]):
        acc = acc + jnp.dot(y_ref[...], w_ref[...], preferred_element_type=F32)
    o_ref[...] = acc


def _mix_out(res, ys, ws, name):
    s, d = res.shape
    tm = min(ROW_BLOCK, s)
    return pl.pallas_call(
        _mix_out_body,
        grid=(s // tm,),
        in_specs=[pl.BlockSpec((tm, d), lambda i: (i, 0))]
        + [pl.BlockSpec((tm, y.shape[1]), lambda i: (i, 0)) for y in ys]
        + [pl.BlockSpec(w.shape, lambda i: (0, 0)) for w in ws],
        out_specs=pl.BlockSpec((tm, d), lambda i: (i, 0)),
        out_shape=jax.ShapeDtypeStruct((s, d), F32),
        compiler_params=_params("parallel"),
        name=name,
    )(res, *ys, *ws)


def _ffn_body(h_ref, g_ref, wup_ref, cw_ref, cb_ref, wdn_ref, fg_ref, o_ref, tail_s, act_s, *, final_norm):
    @pl.when(pl.program_id(0) == 0)
    def _():
        tail_s[...] = jnp.zeros_like(tail_s)

    tm = h_ref.shape[0]
    x = h_ref[...]
    xn = _rms(x, g_ref[...]).astype(BF16)

    def conv(col):
        cs = pl.ds(col, FFN_CHUNK)
        u = jnp.dot(xn, wup_ref[:, cs], preferred_element_type=F32)
        tail = tail_s[:, cs]
        tail_s[:, cs] = u[tm - SUBLANES:, :]
        y = cb_ref[:, cs] + _shift_rows(u, tail, 2) * cw_ref[0:1, cs]
        y = y + _shift_rows(u, tail, 1) * cw_ref[1:2, cs]
        return y + u * cw_ref[2:3, cs]

    for c in range(D_FF // FFN_CHUNK):
        gate = conv(c * FFN_CHUNK)
        val = conv(D_FF + c * FFN_CHUNK)
        act_s[:, pl.ds(c * FFN_CHUNK, FFN_CHUNK)] = (jax.nn.silu(gate) * val).astype(BF16)

    out = x + jnp.dot(act_s[...], wdn_ref[...], preferred_element_type=F32)
    if final_norm:
        out = _rms(out, fg_ref[...])
    o_ref[...] = out


def _ffn(h, g, wup, cw, cb, wdn, fg, final_norm):
    s, d = h.shape
    tm = min(ROW_BLOCK, s)
    const = lambda a: pl.BlockSpec(a.shape, lambda i: (0, 0), pipeline_mode=pl.Buffered(1))
    return pl.pallas_call(
        functools.partial(_ffn_body, final_norm=final_norm),
        grid=(s // tm,),
        in_specs=[pl.BlockSpec((tm, d), lambda i: (i, 0)), const(g), const(wup), const(cw), const(cb),
                  const(wdn), const(fg)],
        out_specs=pl.BlockSpec((tm, d), lambda i: (i, 0)),
        out_shape=jax.ShapeDtypeStruct((s, d), F32),
        scratch_shapes=[pltpu.VMEM((SUBLANES, 2 * D_FF), F32), pltpu.VMEM((tm, D_FF), BF16)],
        compiler_params=_params("arbitrary"),
        name="ffn_final" if final_norm else "ffn",
    )(h, g, wup, cw, cb, wdn, fg)


def _mlstm_body(q_ref, k_ref, v_ref, o_ref, gate_ref, gb_ref, ng_ref, y_ref, c_s, m_s):
    @pl.when(pl.program_id(0) == 0)
    def _():
        c_s[...] = jnp.zeros_like(c_s)
        m_s[...] = jnp.zeros_like(m_s)

    L = q_ref.shape[0]
    capped = GATE_CAP * jnp.tanh((gate_ref[...] + gb_ref[...]) / GATE_CAP)
    log_f = jax.nn.log_sigmoid(capped)
    row = lax.broadcasted_iota(jnp.int32, (L, LANES), 0)
    b_all = log_f
    d = 1
    while d < L:
        b_all = b_all + jnp.where(row >= d, pltpu.roll(b_all, d, axis=0), 0.0)
        d *= 2
    i_t = capped.T
    b_t = b_all.T

    tri = lax.broadcasted_iota(jnp.int32, (L, L), 1) <= lax.broadcasted_iota(jnp.int32, (L, L), 0)
    ones = jnp.ones((L, LANES), BF16)
    q_scale = M_QK ** -0.5
    for h in range(M_HEADS):
        b_col = b_all[:, M_HEADS + h:M_HEADS + h + 1]
        i_col = capped[:, h:h + 1]
        b_row = b_t[M_HEADS + h:M_HEADS + h + 1, :]
        i_row = i_t[h:h + 1, :]
        m_st = m_s[h:h + 1, 0:1]
        c_st = c_s[h]

        dmat = jnp.where(tri, b_col - b_row + i_row, -jnp.inf)
        inter = b_col + m_st
        m_t = jnp.maximum(inter, jnp.max(dmat, axis=-1, keepdims=True))
        w = jnp.exp(dmat - m_t)
        g = jnp.exp(inter - m_t)

        qh = (q_ref[:, h * M_QK:(h + 1) * M_QK] * q_scale).astype(BF16)
        kf = k_ref[:, h * M_QK:(h + 1) * M_QK]
        kh = kf.astype(BF16)
        vaug = jnp.concatenate([v_ref[:, h * M_V:(h + 1) * M_V].astype(BF16), ones], axis=1)
        sc = lax.dot_general(qh, kh, (((1,), (1,)), ((), ())), preferred_element_type=F32) * w
        num_aug = g * jnp.dot(qh, c_st.astype(BF16), preferred_element_type=F32)
        num_aug = num_aug + jnp.dot(sc.astype(BF16), vaug, preferred_element_type=F32)
        den = num_aug[:, M_V:M_V + 1]
        hh = num_aug[:, :M_V] / jnp.maximum(jnp.abs(den), jnp.exp(-m_t))

        b_last = b_col[L - 1:, :]
        w_end = b_last - b_col + i_col
        m_new = jnp.maximum(b_last + m_st, jnp.max(w_end, axis=0, keepdims=True))
        g_end = jnp.exp(b_last + m_st - m_new)
        kw = (kf * jnp.exp(w_end - m_new)).astype(BF16)
        c_s[h] = g_end * c_st + lax.dot_general(kw, vaug, (((0,), (0,)), ((), ())),
                                                preferred_element_type=F32)
        m_s[h:h + 1, :] = jnp.broadcast_to(m_new, (1, LANES))

        hn = hh * lax.rsqrt(jnp.mean(hh * hh, axis=-1, keepdims=True) + EPS)
        vs = slice(h * M_V, (h + 1) * M_V)
        y_ref[:, vs] = (hn * ng_ref[:, vs] * jax.nn.sigmoid(o_ref[:, vs])).astype(y_ref.dtype)


def _mlstm(z, gate_bias, norm_g):
    s = z.shape[0]
    L = min(M_CHUNK, s)
    qk = M_HEADS * M_QK
    vw = M_HEADS * M_V
    return pl.pallas_call(
        _mlstm_body,
        grid=(s // L,),
        in_specs=[
            pl.BlockSpec((L, qk), lambda i: (i, 0)),
            pl.BlockSpec((L, qk), lambda i: (i, 1)),
            pl.BlockSpec((L, vw), lambda i: (i, 2 * qk // vw)),
            pl.BlockSpec((L, vw), lambda i: (i, 2 * qk // vw + 1)),
            pl.BlockSpec((L, LANES), lambda i: (i, ODD_GATE_COL // LANES)),
            pl.BlockSpec((1, LANES), lambda i: (0, 0)),
            pl.BlockSpec((1, vw), lambda i: (0, 0)),
        ],
        out_specs=pl.BlockSpec((L, vw), lambda i: (i, 0)),
        out_shape=jax.ShapeDtypeStruct((s, vw), BF16),
        scratch_shapes=[pltpu.VMEM((M_HEADS, M_QK, M_V + LANES), F32), pltpu.VMEM((SUBLANES, LANES), F32)],
        compiler_params=_params("arbitrary"),
        name="mlstm",
    )(z, z, z, z, z, gate_bias, norm_g)


def _rot_cols(w):
    half = QK_ROPE // 2
    return jnp.concatenate([-w[:, half:], w[:, :half]], axis=1)


def _place(w, start):
    return jnp.pad(w, ((0, 0), (start, HEAD_PAD - start - w.shape[1])))


def _block_diag(w):
    g, bi, bo = w.shape
    eye = jnp.eye(g, dtype=w.dtype)
    return (w[:, :, None, :] * eye[:, None, :, None]).reshape(g * bi, g * bo)


def kernel(x, positions, e_norm_g, e_w_in, e_lru_conv_w, e_lru_conv_b, e_lru_w_a, e_lru_b_a, e_lru_w_x,
           e_lru_b_x, e_lru_lambda, e_q_norm_g, e_w_qb, e_kv_norm_g, e_w_kvb, e_w_out, o_norm_g, o_w_in,
           o_b_igate, o_b_fgate, o_out_norm_g, o_w_out, f_norm_g, f_w_up, f_conv_w, f_conv_b, f_w_down,
           final_norm_g):
    bsz, s, d = x.shape
    assert bsz == 1 and d == D_MODEL
    h = x.reshape(s, d)
    pos = positions.reshape(s, 1)
    depth = f_norm_g.shape[0]

    half = QK_ROPE // 2
    inv_freq = ROPE_THETA ** (-jnp.arange(half, dtype=F32) / half)
    freq = _place(jnp.concatenate([inv_freq, inv_freq])[None, :], QK_NOPE)

    for layer in range(depth):
        j = layer // 2
        if layer % 2 == 0:
            w_in = e_w_in[j]
            c_kr = 2 * LRU_WIDTH + Q_LORA + KV_LORA
            w_kr = w_in[:, c_kr:c_kr + QK_ROPE]
            w_even = jnp.concatenate(
                [w_in[:, :c_kr], _place(w_kr, QK_NOPE), _place(_rot_cols(w_kr), QK_NOPE)], axis=1).astype(BF16)
            z = _norm_matmul(h, e_norm_g[j], w_even, "even_in")

            y_lru = _rglru(
                z, e_lru_conv_w[j], e_lru_conv_b[j][None, :],
                _block_diag(e_lru_w_a[j]).astype(BF16), e_lru_b_a[j].reshape(1, LRU_WIDTH),
                _block_diag(e_lru_w_x[j]).astype(BF16), e_lru_b_x[j].reshape(1, LRU_WIDTH),
                e_lru_lambda[j][None, :])

            wq = e_w_qb[j].reshape(Q_LORA, MLA_HEADS, QK_NOPE + QK_ROPE)
            wq_a = jnp.pad(wq, ((0, 0), (0, 0), (0, HEAD_PAD - QK_NOPE - QK_ROPE)))
            wq_pe = wq[:, :, QK_NOPE:]
            wq_rot = jnp.concatenate([-wq_pe[:, :, half:], wq_pe[:, :, :half]], axis=2)
            wq_b = jnp.pad(wq_rot, ((0, 0), (0, 0), (QK_NOPE, HEAD_PAD - QK_NOPE - QK_ROPE)))
            wkv = e_w_kvb[j].reshape(KV_LORA, MLA_HEADS, QK_NOPE + V_HEAD)
            wk = jnp.pad(wkv[:, :, :QK_NOPE], ((0, 0), (0, 0), (0, HEAD_PAD - QK_NOPE)))
            wv = jnp.pad(wkv[:, :, QK_NOPE:], ((0, 0), (0, 0), (0, HEAD_PAD - V_HEAD)))
            hw = MLA_HEADS * HEAD_PAD
            q, k, v = _mla_prep(
                z, pos, e_q_norm_g[j][None, :], e_kv_norm_g[j][None, :],
                wq_a.reshape(Q_LORA, hw).astype(BF16), wq_b.reshape(Q_LORA, hw).astype(BF16),
                wk.reshape(KV_LORA, hw).astype(BF16), wv.reshape(KV_LORA, hw).astype(BF16), freq)
            y_mla = _attention(q, k, v)

            w_out = e_w_out[j]
            w_o_mla = jnp.pad(w_out[LRU_WIDTH:].reshape(MLA_HEADS, V_HEAD, d),
                              ((0, 0), (0, HEAD_PAD - V_HEAD), (0, 0))).reshape(hw, d)
            h = _mix_out(h, [y_lru, y_mla], [w_out[:LRU_WIDTH].astype(BF16), w_o_mla.astype(BF16)], "even_out")
        else:
            w_odd = jnp.pad(o_w_in[j], ((0, 0), (0, ODD_COLS - o_w_in.shape[2]))).astype(BF16)
            z = _norm_matmul(h, o_norm_g[j], w_odd, "odd_in")
            gate_bias = jnp.pad(jnp.concatenate([o_b_igate[j], o_b_fgate[j]])[None, :],
                                ((0, 0), (0, LANES - 2 * M_HEADS)))
            y = _mlstm(z, gate_bias, o_out_norm_g[j][None, :])
            h = _mix_out(h, [y], [o_w_out[j].astype(BF16)], "odd_out")

        h = _ffn(h, f_norm_g[layer][None, :], f_w_up[layer].astype(BF16), f_conv_w[layer],
                 f_conv_b[layer][None, :], f_w_down[layer].astype(BF16), final_norm_g[None, :],
                 final_norm=(layer == depth - 1))
    return h.reshape(bsz, s, d)
```

```python
import functools

import jax
import jax.numpy as jnp
from jax import lax
from jax.experimental import pallas as pl
from jax.experimental.pallas import tpu as pltpu

F32 = jnp.float32
BF16 = jnp.bfloat16

EPS = 1e-6
LANES = 128
SUBLANES = 8
VMEM_LIMIT = 56 * 1024 * 1024

D_MODEL = 1024
LRU_WIDTH = 512
LRU_BLOCKS = 8
LRU_CONV = 4
LRU_C = 8.0
MLA_HEADS = 8
Q_LORA = 256
KV_LORA = 128
QK_NOPE = 64
QK_ROPE = 32
V_HEAD = 64
ROPE_THETA = 10000.0
HEAD_PAD = 128
M_HEADS = 4
M_QK = 128
M_V = 256
GATE_CAP = 15.0
D_FF = 2816
FFN_CONV = 3

EVEN_COLS = 2 * LRU_WIDTH + Q_LORA + KV_LORA + 2 * HEAD_PAD
ODD_GATE_COL = 2 * M_HEADS * M_QK + 2 * M_HEADS * M_V
ODD_COLS = ODD_GATE_COL + LANES

ROW_BLOCK = 512
ATTN_Q_BLOCK = 1024
ATTN_K_BLOCK = 256
ATTN_ROWS = 32
LOG2E = 1.4426950408889634
M_CHUNK = 128
FFN_CHUNK = 256


def _params(*sem):
    return pltpu.CompilerParams(dimension_semantics=sem, vmem_limit_bytes=VMEM_LIMIT)


def _rms(x, g):
    inv = lax.rsqrt(jnp.mean(x * x, axis=-1, keepdims=True) + EPS)
    return x * inv * g


def _fold_rows(x, op):
    r = x.shape[0]
    while r > SUBLANES:
        r //= 2
        x = op(x[:r], x[r:])
    return x


def _shift_rows(x, prev_tail, sh):
    rolled = pltpu.roll(x, sh, axis=0)
    row = lax.broadcasted_iota(jnp.int32, prev_tail.shape, 0)
    head = jnp.where(row < sh, pltpu.roll(prev_tail, sh, axis=0), rolled[:SUBLANES])
    return jnp.concatenate([head, rolled[SUBLANES:]], axis=0)


def _norm_matmul_body(x_ref, g_ref, w_ref, o_ref):
    xn = _rms(x_ref[...], g_ref[...]).astype(BF16)
    o_ref[...] = jnp.dot(xn, w_ref[...], preferred_element_type=F32)


def _norm_matmul(x, g, w, name):
    s, d = x.shape
    n = w.shape[1]
    tm = min(ROW_BLOCK, s)
    return pl.pallas_call(
        _norm_matmul_body,
        grid=(s // tm,),
        in_specs=[
            pl.BlockSpec((tm, d), lambda i: (i, 0)),
            pl.BlockSpec((1, d), lambda i: (0, 0)),
            pl.BlockSpec((d, n), lambda i: (0, 0)),
        ],
        out_specs=pl.BlockSpec((tm, n), lambda i: (i, 0)),
        out_shape=jax.ShapeDtypeStruct((s, n), F32),
        compiler_params=_params("parallel"),
        name=name,
    )(x, g.reshape(1, d), w)


def _rglru_body(xr_ref, gate_ref, cw_ref, cb_ref, wa_ref, ba_ref, wx_ref, bx_ref, lam_ref,
                o_ref, tail_s, h_s, a_s, b_s):
    @pl.when(pl.program_id(0) == 0)
    def _():
        tail_s[...] = jnp.zeros_like(tail_s)
        h_s[...] = jnp.zeros_like(h_s)

    tm = xr_ref.shape[0]
    xr = xr_ref[...]
    tail = tail_s[...]
    tail_s[...] = xr[tm - SUBLANES:, :]
    xc = cb_ref[...] + _shift_rows(xr, tail, 3) * cw_ref[0:1, :]
    xc = xc + _shift_rows(xr, tail, 2) * cw_ref[1:2, :]
    xc = xc + _shift_rows(xr, tail, 1) * cw_ref[2:3, :]
    xc = xc + xr * cw_ref[3:4, :]

    xcb = xc.astype(BF16)
    r = jax.nn.sigmoid(jnp.dot(xcb, wa_ref[...], preferred_element_type=F32) + ba_ref[...])
    i = jax.nn.sigmoid(jnp.dot(xcb, wx_ref[...], preferred_element_type=F32) + bx_ref[...])
    neg_lam = -lam_ref[...]
    softplus = jnp.maximum(neg_lam, 0.0) + jnp.log1p(jnp.exp(-jnp.abs(neg_lam)))
    log_a = (-LRU_C * r) * softplus
    a = jnp.exp(log_a)
    th = jnp.tanh(log_a)
    u = jnp.sqrt(-2.0 * th / (1.0 - th)) * (i * xc)

    sub = lax.broadcasted_iota(jnp.int32, a.shape, 0) & (SUBLANES - 1)
    for d in (1, 2, 4):
        keep = sub >= d
        a_sh = jnp.where(keep, pltpu.roll(a, d, axis=0), 1.0)
        u_sh = jnp.where(keep, pltpu.roll(u, d, axis=0), 0.0)
        u = a * u_sh + u
        a = a * a_sh
    a_s[...] = a
    b_s[...] = u

    def group(k, h):
        off = pl.multiple_of(k * SUBLANES, SUBLANES)
        ht = a_s[pl.ds(off, SUBLANES), :] * h + b_s[pl.ds(off, SUBLANES), :]
        b_s[pl.ds(off, SUBLANES), :] = ht
        return ht[SUBLANES - 1:, :]

    h_s[...] = lax.fori_loop(0, tm // SUBLANES, group, h_s[...], unroll=8)
    o_ref[...] = (b_s[...] * jax.nn.gelu(gate_ref[...])).astype(o_ref.dtype)


def _rglru(z, cw, cb, wa, ba, wx, bx, lam):
    s = z.shape[0]
    w = LRU_WIDTH
    tm = min(ROW_BLOCK, s)
    vec = lambda r: pl.BlockSpec((r, w), lambda i: (0, 0))
    return pl.pallas_call(
        _rglru_body,
        grid=(s // tm,),
        in_specs=[
            pl.BlockSpec((tm, w), lambda i: (i, 0)),
            pl.BlockSpec((tm, w), lambda i: (i, 1)),
            vec(LRU_CONV), vec(1), vec(w), vec(1), vec(w), vec(1), vec(1),
        ],
        out_specs=pl.BlockSpec((tm, w), lambda i: (i, 0)),
        out_shape=jax.ShapeDtypeStruct((s, w), BF16),
        scratch_shapes=[
            pltpu.VMEM((SUBLANES, w), F32),
            pltpu.VMEM((1, w), F32),
            pltpu.VMEM((tm, w), F32),
            pltpu.VMEM((tm, w), F32),
        ],
        compiler_params=_params("arbitrary"),
        name="rglru",
    )(z, z, cw, cb, wa, ba, wx, bx, lam)


def _mla_prep_body(zq_ref, zkv_ref, zkr_ref, zkrot_ref, pos_ref, qg_ref, kvg_ref,
                   wqa_ref, wqb_ref, wk_ref, wv_ref, freq_ref, qt_ref, k_ref, vt_ref):
    nt = (((1,), (1,)), ((), ()))
    ang = pos_ref[...].astype(F32) * freq_ref[...]
    cosv = jnp.cos(ang)
    sinv = jnp.sin(ang)
    lane = lax.broadcasted_iota(jnp.int32, (1, HEAD_PAD), 1)
    scale = (QK_NOPE + QK_ROPE) ** -0.5 * LOG2E
    cq_t = (jnp.where(lane < QK_NOPE + QK_ROPE, cosv, 0.0) * scale).T
    sq_t = (sinv * scale).T
    ck = jnp.where(lane >= QK_NOPE, cosv, 0.0)

    qn = _rms(zq_ref[...], qg_ref[...]).astype(BF16)
    qa_t = lax.dot_general(wqa_ref[...], qn, nt, preferred_element_type=F32)
    qb_t = lax.dot_general(wqb_ref[...], qn, nt, preferred_element_type=F32)
    kvn = _rms(zkv_ref[...], kvg_ref[...]).astype(BF16)
    kn = jnp.dot(kvn, wk_ref[...], preferred_element_type=F32)
    v_t = lax.dot_general(wv_ref[...], kvn, nt, preferred_element_type=F32).astype(BF16)
    tk = vt_ref.shape[2]
    for c in range(vt_ref.shape[0]):
        vt_ref[c] = v_t[:, c * tk:(c + 1) * tk]
    kpe = zkr_ref[...] * ck + zkrot_ref[...] * sinv
    for h in range(MLA_HEADS):
        hs = slice(h * HEAD_PAD, (h + 1) * HEAD_PAD)
        qt_ref[0, hs, :] = (qa_t[hs, :] * cq_t + qb_t[hs, :] * sq_t).astype(BF16)
        k_ref[:, hs] = (kn[:, hs] + kpe).astype(BF16)


def _mla_prep(z, pos, qg, kvg, wqa_t, wqb_t, wk, wv_t, freq):
    s = z.shape[0]
    tm = min(ATTN_Q_BLOCK, s)
    tk = ATTN_K_BLOCK
    assert s % tm == 0 and tm % (2 * tk) == 0
    nb = s // tm
    hw = MLA_HEADS * HEAD_PAD
    vw = MLA_HEADS * V_HEAD
    c0 = 2 * LRU_WIDTH
    full = lambda a: pl.BlockSpec(a.shape, lambda i: (0, 0))
    return pl.pallas_call(
        _mla_prep_body,
        grid=(nb,),
        in_specs=[
            pl.BlockSpec((tm, Q_LORA), lambda i: (i, c0 // Q_LORA)),
            pl.BlockSpec((tm, KV_LORA), lambda i: (i, (c0 + Q_LORA) // KV_LORA)),
            pl.BlockSpec((tm, HEAD_PAD), lambda i: (i, (c0 + Q_LORA + KV_LORA) // HEAD_PAD)),
            pl.BlockSpec((tm, HEAD_PAD), lambda i: (i, (c0 + Q_LORA + KV_LORA) // HEAD_PAD + 1)),
            pl.BlockSpec((tm, 1), lambda i: (i, 0)),
            full(qg), full(kvg), full(wqa_t), full(wqb_t), full(wk), full(wv_t), full(freq),
        ],
        out_specs=[
            pl.BlockSpec((1, hw, tm), lambda i: (i, 0, 0)),
            pl.BlockSpec((tm, hw), lambda i: (i, 0)),
            pl.BlockSpec((tm // tk, vw, tk), lambda i: (i, 0, 0)),
        ],
        out_shape=[
            jax.ShapeDtypeStruct((nb, hw, tm), BF16),
            jax.ShapeDtypeStruct((s, hw), BF16),
            jax.ShapeDtypeStruct((s // tk, vw, tk), BF16),
        ],
        compiler_params=_params("parallel"),
        name="mla_prep",
    )(z, z, z, z, pos, qg, kvg, wqa_t, wqb_t, wk, wv_t, freq)


def _attn_body(qt_ref, k_ref, vt_ref, o_ref, *scratch):
    ns = ATTN_Q_BLOCK // ATTN_K_BLOCK
    ahead = ns - 2
    s_buf, smax_buf, p_buf, acc_s = scratch[:ns], scratch[ns:2 * ns], scratch[2 * ns:2 * ns + 2], scratch[-1]
    tq = qt_ref.shape[2]
    tk = s_buf[0].shape[0]
    i = pl.program_id(1)
    qt = qt_ref[0]
    neg = jnp.finfo(F32).min

    def qk(b, c):
        off = pl.multiple_of(b * tk, tk)
        s = jnp.dot(k_ref[pl.ds(off, tk), :], qt, preferred_element_type=F32)
        s_buf[c % ns][...] = s
        smax_buf[c % ns][...] = _fold_rows(s, jnp.maximum)

    def softmax(c, m, l, key_off=None):
        sb = s_buf[c % ns]
        chunks = range(0, tk, ATTN_ROWS)

        def load(r):
            s = sb[r:r + ATTN_ROWS, :]
            if key_off is None:
                return s
            key = lax.broadcasted_iota(jnp.int32, s.shape, 0) + (key_off + r)
            return jnp.where(key <= lax.broadcasted_iota(jnp.int32, s.shape, 1), s, neg)

        if key_off is None:
            smax = smax_buf[c % ns][...]
        else:
            smax = functools.reduce(jnp.maximum, [_fold_rows(load(r), jnp.maximum) for r in chunks])
        m_new = jnp.maximum(m, jnp.max(smax, axis=0, keepdims=True))
        psum = jnp.zeros((SUBLANES, tq), F32)
        for r in chunks:
            p = jnp.exp2(load(r) - m_new)
            psum = psum + _fold_rows(p, jnp.add)
            p_buf[c % 2][r:r + ATTN_ROWS, :] = p.astype(BF16)
        alpha = jnp.exp2(m - m_new)
        l = alpha * l + jnp.sum(psum, axis=0, keepdims=True)
        return m_new, l, alpha

    def pv(b, c, alpha):
        acc_s[...] = alpha * acc_s[...] + jnp.dot(vt_ref[b], p_buf[c % 2][...], preferred_element_type=F32)

    def trip(t, carry):
        m, l, alpha = carry
        for c in range(ns):
            b = t * ns + c
            qk(b + ahead, c + ahead)
            pv(jnp.maximum(b - 1, 0), c - 1, alpha)
            m, l, alpha = softmax(c, m, l)
        return m, l, alpha

    for c in range(ahead):
        qk(c, c)
    p_buf[1][...] = jnp.zeros_like(p_buf[1])
    acc_s[...] = jnp.zeros_like(acc_s)
    init = (jnp.full((1, tq), -jnp.inf, F32), jnp.zeros((1, tq), F32), jnp.ones((1, tq), F32))
    m, l, alpha = lax.fori_loop(0, i, trip, init)

    b0 = i * ns
    for c in range(ns):
        if c + ahead < ns:
            qk(b0 + c + ahead, c + ahead)
        pv(jnp.maximum(b0 + c - 1, 0), c - 1, alpha)
        m, l, alpha = softmax(c, m, l, c * tk)
    pv(b0 + ns - 1, ns - 1, alpha)
    o_ref[...] = (acc_s[...] / l).astype(o_ref.dtype)


def _attention(qt, k, vt):
    nb, hw, tq = qt.shape
    nkb, vw, tk = vt.shape
    s = k.shape[0]
    return pl.pallas_call(
        _attn_body,
        grid=(MLA_HEADS, nb),
        in_specs=[
            pl.BlockSpec((1, HEAD_PAD, tq), lambda h, i: (i, h, 0)),
            pl.BlockSpec((s, HEAD_PAD), lambda h, i: (0, h)),
            pl.BlockSpec((nkb, V_HEAD, tk), lambda h, i: (0, h, 0)),
        ],
        out_specs=pl.BlockSpec((V_HEAD, tq), lambda h, i: (h, i)),
        out_shape=jax.ShapeDtypeStruct((vw, s), BF16),
        scratch_shapes=[pltpu.VMEM((tk, tq), F32)] * (tq // tk) + [pltpu.VMEM((SUBLANES, tq), F32)] * (tq // tk)
        + [pltpu.VMEM((tk, tq), BF16)] * 2 + [pltpu.VMEM((V_HEAD, tq), F32)],
        compiler_params=_params("parallel", "parallel"),
        name="attention",
    )(qt, k, vt)


def _mix_out_body(*refs, transposed):
    res_ref, o_ref = refs[0], refs[-1]
    acc = res_ref[...]
    n_in = len(transposed)
    for y_ref, w_ref, tr in zip(refs[1:1 + n_in], refs[1 + n_in:1 + 2 * n_in], transposed):
        dims = (((0,), (0,)), ((), ())) if tr else (((1,), (0,)), ((), ()))
        acc = acc + lax.dot_general(y_ref[...], w_ref[...], dims, preferred_element_type=F32)
    o_ref[...] = acc


def _mix_out(res, ys, ws, transposed, name):
    s, d = res.shape
    tm = min(ROW_BLOCK, s)
    y_specs = [pl.BlockSpec((y.shape[0], tm), lambda i: (0, i)) if tr else
               pl.BlockSpec((tm, y.shape[1]), lambda i: (i, 0)) for y, tr in zip(ys, transposed)]
    return pl.pallas_call(
        functools.partial(_mix_out_body, transposed=transposed),
        grid=(s // tm,),
        in_specs=[pl.BlockSpec((tm, d), lambda i: (i, 0))] + y_specs
        + [pl.BlockSpec(w.shape, lambda i: (0, 0)) for w in ws],
        out_specs=pl.BlockSpec((tm, d), lambda i: (i, 0)),
        out_shape=jax.ShapeDtypeStruct((s, d), F32),
        compiler_params=_params("parallel"),
        name=name,
    )(res, *ys, *ws)


def _ffn_body(h_ref, g_ref, wup_ref, cw_ref, cb_ref, wdn_ref, fg_ref, o_ref, tail_s, act_s, *, final_norm):
    @pl.when(pl.program_id(0) == 0)
    def _():
        tail_s[...] = jnp.zeros_like(tail_s)

    tm = h_ref.shape[0]
    x = h_ref[...]
    xn = _rms(x, g_ref[...]).astype(BF16)

    def conv(col):
        cs = pl.ds(col, FFN_CHUNK)
        u = jnp.dot(xn, wup_ref[:, cs], preferred_element_type=F32)
        tail = tail_s[:, cs]
        tail_s[:, cs] = u[tm - SUBLANES:, :]
        y = cb_ref[:, cs] + _shift_rows(u, tail, 2) * cw_ref[0:1, cs]
        y = y + _shift_rows(u, tail, 1) * cw_ref[1:2, cs]
        return y + u * cw_ref[2:3, cs]

    for c in range(D_FF // FFN_CHUNK):
        gate = conv(c * FFN_CHUNK)
        val = conv(D_FF + c * FFN_CHUNK)
        act_s[:, pl.ds(c * FFN_CHUNK, FFN_CHUNK)] = (jax.nn.silu(gate) * val).astype(BF16)

    out = x + jnp.dot(act_s[...], wdn_ref[...], preferred_element_type=F32)
    if final_norm:
        out = _rms(out, fg_ref[...])
    o_ref[...] = out


def _ffn(h, g, wup, cw, cb, wdn, fg, final_norm):
    s, d = h.shape
    tm = min(ROW_BLOCK, s)
    const = lambda a: pl.BlockSpec(a.shape, lambda i: (0, 0), pipeline_mode=pl.Buffered(1))
    return pl.pallas_call(
        functools.partial(_ffn_body, final_norm=final_norm),
        grid=(s // tm,),
        in_specs=[pl.BlockSpec((tm, d), lambda i: (i, 0)), const(g), const(wup), const(cw), const(cb),
                  const(wdn), const(fg)],
        out_specs=pl.BlockSpec((tm, d), lambda i: (i, 0)),
        out_shape=jax.ShapeDtypeStruct((s, d), F32),
        scratch_shapes=[pltpu.VMEM((SUBLANES, 2 * D_FF), F32), pltpu.VMEM((tm, D_FF), BF16)],
        compiler_params=_params("arbitrary"),
        name="ffn_final" if final_norm else "ffn",
    )(h, g, wup, cw, cb, wdn, fg)


def _mlstm_body(q_ref, k_ref, v_ref, o_ref, gate_ref, gb_ref, ng_ref, y_ref, c_s, m_s):
    @pl.when(pl.program_id(0) == 0)
    def _():
        c_s[...] = jnp.zeros_like(c_s)
        m_s[...] = jnp.zeros_like(m_s)

    L = q_ref.shape[0]
    capped = GATE_CAP * jnp.tanh((gate_ref[...] + gb_ref[...]) / GATE_CAP)
    log_f = jax.nn.log_sigmoid(capped)
    row = lax.broadcasted_iota(jnp.int32, (L, LANES), 0)
    b_all = log_f
    d = 1
    while d < L:
        b_all = b_all + jnp.where(row >= d, pltpu.roll(b_all, d, axis=0), 0.0)
        d *= 2
    i_t = capped.T
    b_t = b_all.T

    tri = lax.broadcasted_iota(jnp.int32, (L, L), 1) <= lax.broadcasted_iota(jnp.int32, (L, L), 0)
    ones = jnp.ones((L, LANES), BF16)
    q_scale = M_QK ** -0.5
    for h in range(M_HEADS):
        b_col = b_all[:, M_HEADS + h:M_HEADS + h + 1]
        i_col = capped[:, h:h + 1]
        b_row = b_t[M_HEADS + h:M_HEADS + h + 1, :]
        i_row = i_t[h:h + 1, :]
        m_st = m_s[h:h + 1, 0:1]
        c_st = c_s[h]

        dmat = jnp.where(tri, b_col - b_row + i_row, -jnp.inf)
        inter = b_col + m_st
        m_t = jnp.maximum(inter, jnp.max(dmat, axis=-1, keepdims=True))
        w = jnp.exp(dmat - m_t)
        g = jnp.exp(inter - m_t)

        qh = (q_ref[:, h * M_QK:(h + 1) * M_QK] * q_scale).astype(BF16)
        kf = k_ref[:, h * M_QK:(h + 1) * M_QK]
        kh = kf.astype(BF16)
        vaug = jnp.concatenate([v_ref[:, h * M_V:(h + 1) * M_V].astype(BF16), ones], axis=1)
        sc = lax.dot_general(qh, kh, (((1,), (1,)), ((), ())), preferred_element_type=F32) * w
        num_aug = g * jnp.dot(qh, c_st.astype(BF16), preferred_element_type=F32)
        num_aug = num_aug + jnp.dot(sc.astype(BF16), vaug, preferred_element_type=F32)
        den = num_aug[:, M_V:M_V + 1]
        hh = num_aug[:, :M_V] / jnp.maximum(jnp.abs(den), jnp.exp(-m_t))

        b_last = b_col[L - 1:, :]
        w_end = b_last - b_col + i_col
        m_new = jnp.maximum(b_last + m_st, jnp.max(w_end, axis=0, keepdims=True))
        g_end = jnp.exp(b_last + m_st - m_new)
        kw = (kf * jnp.exp(w_end - m_new)).astype(BF16)
        c_s[h] = g_end * c_st + lax.dot_general(kw, vaug, (((0,), (0,)), ((), ())),
                                                preferred_element_type=F32)
        m_s[h:h + 1, :] = jnp.broadcast_to(m_new, (1, LANES))

        hn = hh * lax.rsqrt(jnp.mean(hh * hh, axis=-1, keepdims=True) + EPS)
        vs = slice(h * M_V, (h + 1) * M_V)
        y_ref[:, vs] = (hn * ng_ref[:, vs] * jax.nn.sigmoid(o_ref[:, vs])).astype(y_ref.dtype)


def _mlstm(z, gate_bias, norm_g):
    s = z.shape[0]
    L = min(M_CHUNK, s)
    qk = M_HEADS * M_QK
    vw = M_HEADS * M_V
    return pl.pallas_call(
        _mlstm_body,
        grid=(s // L,),
        in_specs=[
            pl.BlockSpec((L, qk), lambda i: (i, 0)),
            pl.BlockSpec((L, qk), lambda i: (i, 1)),
            pl.BlockSpec((L, vw), lambda i: (i, 2 * qk // vw)),
            pl.BlockSpec((L, vw), lambda i: (i, 2 * qk // vw + 1)),
            pl.BlockSpec((L, LANES), lambda i: (i, ODD_GATE_COL // LANES)),
            pl.BlockSpec((1, LANES), lambda i: (0, 0)),
            pl.BlockSpec((1, vw), lambda i: (0, 0)),
        ],
        out_specs=pl.BlockSpec((L, vw), lambda i: (i, 0)),
        out_shape=jax.ShapeDtypeStruct((s, vw), BF16),
        scratch_shapes=[pltpu.VMEM((M_HEADS, M_QK, M_V + LANES), F32), pltpu.VMEM((SUBLANES, LANES), F32)],
        compiler_params=_params("arbitrary"),
        name="mlstm",
    )(z, z, z, z, z, gate_bias, norm_g)


def _rot_cols(w):
    half = QK_ROPE // 2
    return jnp.concatenate([-w[:, half:], w[:, :half]], axis=1)


def _place(w, start):
    return jnp.pad(w, ((0, 0), (start, HEAD_PAD - start - w.shape[1])))


def _block_diag(w):
    g, bi, bo = w.shape
    eye = jnp.eye(g, dtype=w.dtype)
    return (w[:, :, None, :] * eye[:, None, :, None]).reshape(g * bi, g * bo)


def kernel(x, positions, e_norm_g, e_w_in, e_lru_conv_w, e_lru_conv_b, e_lru_w_a, e_lru_b_a, e_lru_w_x,
           e_lru_b_x, e_lru_lambda, e_q_norm_g, e_w_qb, e_kv_norm_g, e_w_kvb, e_w_out, o_norm_g, o_w_in,
           o_b_igate, o_b_fgate, o_out_norm_g, o_w_out, f_norm_g, f_w_up, f_conv_w, f_conv_b, f_w_down,
           final_norm_g):
    bsz, s, d = x.shape
    assert bsz == 1 and d == D_MODEL
    h = x.reshape(s, d)
    pos = positions.reshape(s, 1)
    depth = f_norm_g.shape[0]

    half = QK_ROPE // 2
    inv_freq = ROPE_THETA ** (-jnp.arange(half, dtype=F32) / half)
    freq = _place(jnp.concatenate([inv_freq, inv_freq])[None, :], QK_NOPE)

    for layer in range(depth):
        j = layer // 2
        if layer % 2 == 0:
            w_in = e_w_in[j]
            c_kr = 2 * LRU_WIDTH + Q_LORA + KV_LORA
            w_kr = w_in[:, c_kr:c_kr + QK_ROPE]
            w_even = jnp.concatenate(
                [w_in[:, :c_kr], _place(w_kr, QK_NOPE), _place(_rot_cols(w_kr), QK_NOPE)], axis=1).astype(BF16)
            z = _norm_matmul(h, e_norm_g[j], w_even, "even_in")

            y_lru = _rglru(
                z, e_lru_conv_w[j], e_lru_conv_b[j][None, :],
                _block_diag(e_lru_w_a[j]).astype(BF16), e_lru_b_a[j].reshape(1, LRU_WIDTH),
                _block_diag(e_lru_w_x[j]).astype(BF16), e_lru_b_x[j].reshape(1, LRU_WIDTH),
                e_lru_lambda[j][None, :])

            wq = e_w_qb[j].reshape(Q_LORA, MLA_HEADS, QK_NOPE + QK_ROPE)
            wq_a = jnp.pad(wq, ((0, 0), (0, 0), (0, HEAD_PAD - QK_NOPE - QK_ROPE)))
            wq_pe = wq[:, :, QK_NOPE:]
            wq_rot = jnp.concatenate([-wq_pe[:, :, half:], wq_pe[:, :, :half]], axis=2)
            wq_b = jnp.pad(wq_rot, ((0, 0), (0, 0), (QK_NOPE, HEAD_PAD - QK_NOPE - QK_ROPE)))
            wkv = e_w_kvb[j].reshape(KV_LORA, MLA_HEADS, QK_NOPE + V_HEAD)
            wk = jnp.pad(wkv[:, :, :QK_NOPE], ((0, 0), (0, 0), (0, HEAD_PAD - QK_NOPE)))
            hw = MLA_HEADS * HEAD_PAD
            qt, k, vt = _mla_prep(
                z, pos, e_q_norm_g[j][None, :], e_kv_norm_g[j][None, :],
                wq_a.reshape(Q_LORA, hw).T.astype(BF16), wq_b.reshape(Q_LORA, hw).T.astype(BF16),
                wk.reshape(KV_LORA, hw).astype(BF16),
                wkv[:, :, QK_NOPE:].reshape(KV_LORA, MLA_HEADS * V_HEAD).T.astype(BF16), freq)
            y_mla_t = _attention(qt, k, vt)

            w_out = e_w_out[j].astype(BF16)
            h = _mix_out(h, [y_lru, y_mla_t], [w_out[:LRU_WIDTH], w_out[LRU_WIDTH:]], (False, True), "even_out")
        else:
            w_odd = jnp.pad(o_w_in[j], ((0, 0), (0, ODD_COLS - o_w_in.shape[2]))).astype(BF16)
            z = _norm_matmul(h, o_norm_g[j], w_odd, "odd_in")
            gate_bias = jnp.pad(jnp.concatenate([o_b_igate[j], o_b_fgate[j]])[None, :],
                                ((0, 0), (0, LANES - 2 * M_HEADS)))
            y = _mlstm(z, gate_bias, o_out_norm_g[j][None, :])
            h = _mix_out(h, [y], [o_w_out[j].astype(BF16)], (False,), "odd_out")

        h = _ffn(h, f_norm_g[layer][None, :], f_w_up[layer].astype(BF16), f_conv_w[layer],
                 f_conv_b[layer][None, :], f_w_down[layer].astype(BF16), final_norm_g[None, :],
                 final_norm=(layer == depth - 1))
    return h.reshape(bsz, s, d)
```

```python
import functools

import jax
import jax.numpy as jnp
from jax import lax
from jax.experimental import pallas as pl
from jax.experimental.pallas import tpu as pltpu

F32 = jnp.float32
BF16 = jnp.bfloat16

EPS = 1e-6
LANES = 128
SUBLANES = 8
VMEM_LIMIT = 56 * 1024 * 1024

D_MODEL = 1024
LRU_WIDTH = 512
LRU_BLOCKS = 8
LRU_CONV = 4
LRU_C = 8.0
MLA_HEADS = 8
Q_LORA = 256
KV_LORA = 128
QK_NOPE = 64
QK_ROPE = 32
V_HEAD = 64
V_AUG = 80
ROPE_THETA = 10000.0
HEAD_PAD = 128
M_HEADS = 4
M_QK = 128
M_V = 256
GATE_CAP = 15.0
D_FF = 2816
FFN_CONV = 3

EVEN_COLS = 2 * LRU_WIDTH + Q_LORA + KV_LORA + 2 * HEAD_PAD
ODD_QKV_COLS = 2 * M_HEADS * M_QK + M_HEADS * M_V
ODD_COLS = ODD_QKV_COLS + M_HEADS * M_V + LANES

ROW_BLOCK = 512
ATTN_Q_BLOCK = 1024
ATTN_K_BLOCK = 256
ATTN_ROWS = 32
LOG2E = 1.4426950408889634
M_CHUNK = 256
FFN_CHUNK = 256


def _params(*sem):
    return pltpu.CompilerParams(dimension_semantics=sem, vmem_limit_bytes=VMEM_LIMIT)


def _rms(x, g):
    inv = lax.rsqrt(jnp.mean(x * x, axis=-1, keepdims=True) + EPS)
    return x * inv * g


def _fold_rows(x, op):
    r = x.shape[0]
    while r > SUBLANES:
        r //= 2
        x = op(x[:r], x[r:])
    return x


def _shift_rows(x, prev_tail, sh):
    rolled = pltpu.roll(x, sh, axis=0)
    row = lax.broadcasted_iota(jnp.int32, prev_tail.shape, 0)
    head = jnp.where(row < sh, pltpu.roll(prev_tail, sh, axis=0), rolled[:SUBLANES])
    return jnp.concatenate([head, rolled[SUBLANES:]], axis=0)


def _norm_matmul_body(x_ref, g_ref, w_ref, *o_refs):
    xn = _rms(x_ref[...], g_ref[...]).astype(BF16)
    col = 0
    for o_ref in o_refs:
        n = o_ref.shape[1]
        o_ref[...] = jnp.dot(xn, w_ref[:, col:col + n], preferred_element_type=F32).astype(o_ref.dtype)
        col += n


def _norm_matmul(x, g, w, name, bf16_cols=0):
    s, d = x.shape
    n = w.shape[1]
    tm = min(ROW_BLOCK, s)
    widths = [(bf16_cols, BF16), (n - bf16_cols, F32)] if bf16_cols else [(n, F32)]
    return pl.pallas_call(
        _norm_matmul_body,
        grid=(s // tm,),
        in_specs=[
            pl.BlockSpec((tm, d), lambda i: (i, 0)),
            pl.BlockSpec((1, d), lambda i: (0, 0)),
            pl.BlockSpec((d, n), lambda i: (0, 0)),
        ],
        out_specs=[pl.BlockSpec((tm, c), lambda i: (i, 0)) for c, _ in widths],
        out_shape=[jax.ShapeDtypeStruct((s, c), dt) for c, dt in widths],
        compiler_params=_params("parallel"),
        name=name,
    )(x, g.reshape(1, d), w)


def _rglru_body(xr_ref, gate_ref, cw_ref, cb_ref, wa_ref, ba_ref, wx_ref, bx_ref, lam_ref,
                o_ref, tail_s, h_s, a_s, b_s):
    @pl.when(pl.program_id(0) == 0)
    def _():
        tail_s[...] = jnp.zeros_like(tail_s)
        h_s[...] = jnp.zeros_like(h_s)

    tm = xr_ref.shape[0]
    xr = xr_ref[...]
    tail = tail_s[...]
    tail_s[...] = xr[tm - SUBLANES:, :]
    xc = cb_ref[...] + _shift_rows(xr, tail, 3) * cw_ref[0:1, :]
    xc = xc + _shift_rows(xr, tail, 2) * cw_ref[1:2, :]
    xc = xc + _shift_rows(xr, tail, 1) * cw_ref[2:3, :]
    xc = xc + xr * cw_ref[3:4, :]

    xcb = xc.astype(BF16)
    r = jax.nn.sigmoid(jnp.dot(xcb, wa_ref[...], preferred_element_type=F32) + ba_ref[...])
    i = jax.nn.sigmoid(jnp.dot(xcb, wx_ref[...], preferred_element_type=F32) + bx_ref[...])
    neg_lam = -lam_ref[...]
    softplus = jnp.maximum(neg_lam, 0.0) + jnp.log1p(jnp.exp(-jnp.abs(neg_lam)))
    log_a = (-LRU_C * r) * softplus
    a = jnp.exp(log_a)
    th = jnp.tanh(log_a)
    u = jnp.sqrt(-2.0 * th / (1.0 - th)) * (i * xc)

    sub = lax.broadcasted_iota(jnp.int32, a.shape, 0) & (SUBLANES - 1)
    for d in (1, 2, 4):
        keep = sub >= d
        a_sh = jnp.where(keep, pltpu.roll(a, d, axis=0), 1.0)
        u_sh = jnp.where(keep, pltpu.roll(u, d, axis=0), 0.0)
        u = a * u_sh + u
        a = a * a_sh
    a_s[...] = a
    b_s[...] = u

    def group(k, h):
        off = pl.multiple_of(k * SUBLANES, SUBLANES)
        ht = a_s[pl.ds(off, SUBLANES), :] * h + b_s[pl.ds(off, SUBLANES), :]
        b_s[pl.ds(off, SUBLANES), :] = ht
        return ht[SUBLANES - 1:, :]

    h_s[...] = lax.fori_loop(0, tm // SUBLANES, group, h_s[...], unroll=8)
    o_ref[...] = (b_s[...] * jax.nn.gelu(gate_ref[...])).astype(o_ref.dtype)


def _rglru(z, cw, cb, wa, ba, wx, bx, lam):
    s = z.shape[0]
    w = LRU_WIDTH
    tm = min(ROW_BLOCK, s)
    vec = lambda r: pl.BlockSpec((r, w), lambda i: (0, 0))
    return pl.pallas_call(
        _rglru_body,
        grid=(s // tm,),
        in_specs=[
            pl.BlockSpec((tm, w), lambda i: (i, 0)),
            pl.BlockSpec((tm, w), lambda i: (i, 1)),
            vec(LRU_CONV), vec(1), vec(w), vec(1), vec(w), vec(1), vec(1),
        ],
        out_specs=pl.BlockSpec((tm, w), lambda i: (i, 0)),
        out_shape=jax.ShapeDtypeStruct((s, w), BF16),
        scratch_shapes=[
            pltpu.VMEM((SUBLANES, w), F32),
            pltpu.VMEM((1, w), F32),
            pltpu.VMEM((tm, w), F32),
            pltpu.VMEM((tm, w), F32),
        ],
        compiler_params=_params("arbitrary"),
        name="rglru",
    )(z, z, cw, cb, wa, ba, wx, bx, lam)


def _mla_prep_body(zq_ref, zkv_ref, zkr_ref, zkrot_ref, pos_ref, qg_ref, kvg_ref,
                   wqa_ref, wqb_ref, wk_ref, wv_ref, freq_ref, qt_ref, k_ref, vt_ref):
    nt = (((1,), (1,)), ((), ()))
    ang = pos_ref[...].astype(F32) * freq_ref[...]
    cosv = jnp.cos(ang)
    sinv = jnp.sin(ang)
    lane = lax.broadcasted_iota(jnp.int32, (1, HEAD_PAD), 1)
    scale = (QK_NOPE + QK_ROPE) ** -0.5 * LOG2E
    cq_t = (jnp.where(lane < QK_NOPE + QK_ROPE, cosv, 0.0) * scale).T
    sq_t = (sinv * scale).T
    ck = jnp.where(lane >= QK_NOPE, cosv, 0.0)

    qn = _rms(zq_ref[...], qg_ref[...]).astype(BF16)
    qa_t = lax.dot_general(wqa_ref[...], qn, nt, preferred_element_type=F32)
    qb_t = lax.dot_general(wqb_ref[...], qn, nt, preferred_element_type=F32)
    kvn = _rms(zkv_ref[...], kvg_ref[...]).astype(BF16)
    kn = jnp.dot(kvn, wk_ref[...], preferred_element_type=F32)
    v_t = lax.dot_general(wv_ref[...], kvn, nt, preferred_element_type=F32).astype(BF16)
    tk = vt_ref.shape[2]
    ones = jnp.ones((V_AUG - V_HEAD, tk), BF16)
    for c in range(vt_ref.shape[0]):
        for h in range(MLA_HEADS):
            vt_ref[c, h * V_AUG:h * V_AUG + V_HEAD, :] = v_t[h * V_HEAD:(h + 1) * V_HEAD, c * tk:(c + 1) * tk]
            vt_ref[c, h * V_AUG + V_HEAD:(h + 1) * V_AUG, :] = ones
    kpe = zkr_ref[...] * ck + zkrot_ref[...] * sinv
    for h in range(MLA_HEADS):
        hs = slice(h * HEAD_PAD, (h + 1) * HEAD_PAD)
        qt_ref[0, hs, :] = (qa_t[hs, :] * cq_t + qb_t[hs, :] * sq_t).astype(BF16)
        k_ref[:, hs] = (kn[:, hs] + kpe).astype(BF16)


def _mla_prep(z, pos, qg, kvg, wqa_t, wqb_t, wk, wv_t, freq):
    s = z.shape[0]
    tm = min(ATTN_Q_BLOCK, s)
    tk = ATTN_K_BLOCK
    assert s % tm == 0 and tm % (2 * tk) == 0
    nb = s // tm
    hw = MLA_HEADS * HEAD_PAD
    vw = MLA_HEADS * V_AUG
    c0 = 2 * LRU_WIDTH
    full = lambda a: pl.BlockSpec(a.shape, lambda i: (0, 0))
    return pl.pallas_call(
        _mla_prep_body,
        grid=(nb,),
        in_specs=[
            pl.BlockSpec((tm, Q_LORA), lambda i: (i, c0 // Q_LORA)),
            pl.BlockSpec((tm, KV_LORA), lambda i: (i, (c0 + Q_LORA) // KV_LORA)),
            pl.BlockSpec((tm, HEAD_PAD), lambda i: (i, (c0 + Q_LORA + KV_LORA) // HEAD_PAD)),
            pl.BlockSpec((tm, HEAD_PAD), lambda i: (i, (c0 + Q_LORA + KV_LORA) // HEAD_PAD + 1)),
            pl.BlockSpec((tm, 1), lambda i: (i, 0)),
            full(qg), full(kvg), full(wqa_t), full(wqb_t), full(wk), full(wv_t), full(freq),
        ],
        out_specs=[
            pl.BlockSpec((1, hw, tm), lambda i: (i, 0, 0)),
            pl.BlockSpec((tm, hw), lambda i: (i, 0)),
            pl.BlockSpec((tm // tk, vw, tk), lambda i: (i, 0, 0)),
        ],
        out_shape=[
            jax.ShapeDtypeStruct((nb, hw, tm), BF16),
            jax.ShapeDtypeStruct((s, hw), BF16),
            jax.ShapeDtypeStruct((s // tk, vw, tk), BF16),
        ],
        compiler_params=_params("parallel"),
        name="mla_prep",
    )(z, z, z, z, pos, qg, kvg, wqa_t, wqb_t, wk, wv_t, freq)


def _attn_body(qt_ref, k_ref, vt_ref, o_ref, *scratch):
    ns = ATTN_Q_BLOCK // ATTN_K_BLOCK
    ahead = ns - 2
    s_buf, smax_buf, p_buf, acc_s = scratch[:ns], scratch[ns:2 * ns], scratch[2 * ns:2 * ns + 2], scratch[-1]
    tq = qt_ref.shape[2]
    tk = s_buf[0].shape[0]
    i = pl.program_id(1)
    qt = qt_ref[0]
    neg = jnp.finfo(F32).min

    def qk(b, c, lane0=0):
        off = pl.multiple_of(b * tk, tk)
        s = jnp.dot(k_ref[pl.ds(off, tk), :], qt[:, lane0:], preferred_element_type=F32)
        s_buf[c % ns][:, lane0:] = s
        smax_buf[c % ns][:, lane0:] = _fold_rows(s, jnp.maximum)

    def softmax(c, m, key_off=None, lane0=0):
        sb = s_buf[c % ns]
        chunks = range(0, tk, ATTN_ROWS)

        def load(r):
            s = sb[r:r + ATTN_ROWS, lane0:]
            if key_off is None:
                return s
            key = lax.broadcasted_iota(jnp.int32, s.shape, 0) + (key_off + r)
            qry = lax.broadcasted_iota(jnp.int32, s.shape, 1) + lane0
            return jnp.where(key <= qry, s, neg)

        if key_off is None:
            smax = smax_buf[c % ns][...]
        else:
            smax = functools.reduce(jnp.maximum, [_fold_rows(load(r), jnp.maximum) for r in chunks])
        m_old = m[:, lane0:]
        m_new = jnp.maximum(m_old, jnp.max(smax, axis=0, keepdims=True))
        for r in chunks:
            p_buf[c % 2][r:r + ATTN_ROWS, lane0:] = jnp.exp2(load(r) - m_new).astype(BF16)
        alpha = jnp.exp2(m_old - m_new)
        if lane0:
            p_buf[c % 2][:, :lane0] = jnp.zeros((tk, lane0), BF16)
            m_new = jnp.concatenate([m[:, :lane0], m_new], axis=1)
            alpha = jnp.concatenate([jnp.ones((1, lane0), F32), alpha], axis=1)
        return m_new, alpha

    def pv(b, c, alpha):
        acc_s[...] = alpha * acc_s[...] + jnp.dot(vt_ref[b], p_buf[c % 2][...], preferred_element_type=F32)

    def trip(t, carry):
        m, alpha = carry
        for c in range(ns):
            b = t * ns + c
            qk(b + ahead, c + ahead)
            pv(jnp.maximum(b - 1, 0), c - 1, alpha)
            m, alpha = softmax(c, m)
        return m, alpha

    for c in range(ahead):
        qk(c, c)
    p_buf[1][...] = jnp.zeros_like(p_buf[1])
    acc_s[...] = jnp.zeros_like(acc_s)
    m, alpha = lax.fori_loop(0, i, trip, (jnp.full((1, tq), -jnp.inf, F32), jnp.ones((1, tq), F32)))

    b0 = i * ns
    for c in range(ns):
        if c + ahead < ns:
            qk(b0 + c + ahead, c + ahead, (c + ahead) * tk)
        pv(jnp.maximum(b0 + c - 1, 0), c - 1, alpha)
        m, alpha = softmax(c, m, c * tk, c * tk)
    pv(b0 + ns - 1, ns - 1, alpha)
    o_ref[...] = (acc_s[:V_HEAD, :] / acc_s[V_HEAD:V_HEAD + 1, :]).astype(o_ref.dtype)


def _attention(qt, k, vt):
    nb, hw, tq = qt.shape
    nkb, vw, tk = vt.shape
    s = k.shape[0]
    return pl.pallas_call(
        _attn_body,
        grid=(MLA_HEADS, nb),
        in_specs=[
            pl.BlockSpec((1, HEAD_PAD, tq), lambda h, i: (i, h, 0)),
            pl.BlockSpec((s, HEAD_PAD), lambda h, i: (0, h)),
            pl.BlockSpec((nkb, V_AUG, tk), lambda h, i: (0, h, 0)),
        ],
        out_specs=pl.BlockSpec((V_HEAD, tq), lambda h, i: (h, i)),
        out_shape=jax.ShapeDtypeStruct((MLA_HEADS * V_HEAD, s), BF16),
        scratch_shapes=[pltpu.VMEM((tk, tq), F32)] * (tq // tk) + [pltpu.VMEM((SUBLANES, tq), F32)] * (tq // tk)
        + [pltpu.VMEM((tk, tq), BF16)] * 2 + [pltpu.VMEM((V_AUG, tq), F32)],
        compiler_params=_params("parallel", "parallel"),
        name="attention",
    )(qt, k, vt)


def _ffn_body(*refs, transposed, final_norm):
    n_mix = len(transposed)
    res_ref = refs[0]
    y_refs, w_refs = refs[1:1 + n_mix], refs[1 + n_mix:1 + 2 * n_mix]
    g_ref, wup_ref, cw_ref, cb_ref, wdn_ref, fg_ref, o_ref, tail_s, act_s = refs[1 + 2 * n_mix:]

    @pl.when(pl.program_id(0) == 0)
    def _():
        tail_s[...] = jnp.zeros_like(tail_s)

    tm = res_ref.shape[0]
    x = res_ref[...]
    for y_ref, w_ref, tr in zip(y_refs, w_refs, transposed):
        dims = (((0,), (0,)), ((), ())) if tr else (((1,), (0,)), ((), ()))
        x = x + lax.dot_general(y_ref[...], w_ref[...], dims, preferred_element_type=F32)
    xn = _rms(x, g_ref[...]).astype(BF16)

    def conv(col):
        cs = pl.ds(col, FFN_CHUNK)
        u = jnp.dot(xn, wup_ref[:, cs], preferred_element_type=F32)
        tail = tail_s[:, cs]
        tail_s[:, cs] = u[tm - SUBLANES:, :]
        y = cb_ref[:, cs] + _shift_rows(u, tail, 2) * cw_ref[0:1, cs]
        y = y + _shift_rows(u, tail, 1) * cw_ref[1:2, cs]
        return y + u * cw_ref[2:3, cs]

    for c in range(D_FF // FFN_CHUNK):
        gate = conv(c * FFN_CHUNK)
        val = conv(D_FF + c * FFN_CHUNK)
        act_s[:, pl.ds(c * FFN_CHUNK, FFN_CHUNK)] = (jax.nn.silu(gate) * val).astype(BF16)

    out = x + jnp.dot(act_s[...], wdn_ref[...], preferred_element_type=F32)
    if final_norm:
        out = _rms(out, fg_ref[...])
    o_ref[...] = out


def _ffn(res, ys, ws, transposed, g, wup, cw, cb, wdn, fg, final_norm):
    s, d = res.shape
    tm = min(ROW_BLOCK, s)
    const = lambda a: pl.BlockSpec(a.shape, lambda i: (0, 0), pipeline_mode=pl.Buffered(1))
    y_specs = [pl.BlockSpec((y.shape[0], tm), lambda i: (0, i)) if tr else
               pl.BlockSpec((tm, y.shape[1]), lambda i: (i, 0)) for y, tr in zip(ys, transposed)]
    return pl.pallas_call(
        functools.partial(_ffn_body, transposed=transposed, final_norm=final_norm),
        grid=(s // tm,),
        in_specs=[pl.BlockSpec((tm, d), lambda i: (i, 0))] + y_specs + [const(w) for w in ws]
        + [const(g), const(wup), const(cw), const(cb), const(wdn), const(fg)],
        out_specs=pl.BlockSpec((tm, d), lambda i: (i, 0)),
        out_shape=jax.ShapeDtypeStruct((s, d), F32),
        scratch_shapes=[pltpu.VMEM((SUBLANES, 2 * D_FF), F32), pltpu.VMEM((tm, D_FF), BF16)],
        compiler_params=_params("arbitrary"),
        name="ffn_final" if final_norm else "ffn",
    )(res, *ys, *ws, g, wup, cw, cb, wdn, fg)


def _mlstm_body(q_ref, k_ref, v_ref, o_ref, gate_ref, gb_ref, ng_ref, y_ref, c_s, m_s):
    @pl.when(pl.program_id(0) == 0)
    def _():
        c_s[...] = jnp.zeros_like(c_s)
        m_s[...] = jnp.zeros_like(m_s)

    L = q_ref.shape[0]
    capped = GATE_CAP * jnp.tanh((gate_ref[...] + gb_ref[...]) / GATE_CAP)
    log_f = jax.nn.log_sigmoid(capped)
    row = lax.broadcasted_iota(jnp.int32, (L, LANES), 0)
    b_all = log_f
    d = 1
    while d < L:
        b_all = b_all + jnp.where(row >= d, pltpu.roll(b_all, d, axis=0), 0.0)
        d *= 2
    i_t = capped.T
    b_t = b_all.T

    tri = lax.broadcasted_iota(jnp.int32, (L, L), 1) <= lax.broadcasted_iota(jnp.int32, (L, L), 0)
    ones = jnp.ones((L, LANES), BF16)
    q_scale = M_QK ** -0.5
    for h in range(M_HEADS):
        b_col = b_all[:, M_HEADS + h:M_HEADS + h + 1]
        i_col = capped[:, h:h + 1]
        b_row = b_t[M_HEADS + h:M_HEADS + h + 1, :]
        i_row = i_t[h:h + 1, :]
        m_st = m_s[h:h + 1, 0:1]
        c_st = c_s[h]

        dmat = jnp.where(tri, b_col - b_row + i_row, -jnp.inf)
        inter = b_col + m_st
        m_t = jnp.maximum(inter, jnp.max(dmat, axis=-1, keepdims=True))
        w = jnp.exp(dmat - m_t)
        g = jnp.exp(inter - m_t)

        qh = q_ref[:, h * M_QK:(h + 1) * M_QK]
        kh = k_ref[:, h * M_QK:(h + 1) * M_QK]
        vaug = jnp.concatenate([v_ref[:, h * M_V:(h + 1) * M_V], ones], axis=1)
        sc = lax.dot_general(qh, kh, (((1,), (1,)), ((), ())), preferred_element_type=F32) * (w * q_scale)
        num_aug = (g * q_scale) * jnp.dot(qh, c_st.astype(BF16), preferred_element_type=F32)
        num_aug = num_aug + jnp.dot(sc.astype(BF16), vaug, preferred_element_type=F32)
        den = num_aug[:, M_V:M_V + 1]
        hh = num_aug[:, :M_V] / jnp.maximum(jnp.abs(den), jnp.exp(-m_t))

        b_last = b_col[L - 1:, :]
        w_end = b_last - b_col + i_col
        m_new = jnp.maximum(b_last + m_st, jnp.max(w_end, axis=0, keepdims=True))
        g_end = jnp.exp(b_last + m_st - m_new)
        kw = (kh.astype(F32) * jnp.exp(w_end - m_new)).astype(BF16)
        c_s[h] = g_end * c_st + lax.dot_general(kw, vaug, (((0,), (0,)), ((), ())),
                                                preferred_element_type=F32)
        m_s[h:h + 1, :] = jnp.broadcast_to(m_new, (1, LANES))

        hn = hh * lax.rsqrt(jnp.mean(hh * hh, axis=-1, keepdims=True) + EPS)
        vs = slice(h * M_V, (h + 1) * M_V)
        y_ref[:, vs] = (hn * ng_ref[:, vs] * jax.nn.sigmoid(o_ref[:, vs])).astype(y_ref.dtype)


def _mlstm(qkv, og, gate_bias, norm_g):
    s = qkv.shape[0]
    L = min(M_CHUNK, s)
    qk = M_HEADS * M_QK
    vw = M_HEADS * M_V
    return pl.pallas_call(
        _mlstm_body,
        grid=(s // L,),
        in_specs=[
            pl.BlockSpec((L, qk), lambda i: (i, 0)),
            pl.BlockSpec((L, qk), lambda i: (i, 1)),
            pl.BlockSpec((L, vw), lambda i: (i, 2 * qk // vw)),
            pl.BlockSpec((L, vw), lambda i: (i, 0)),
            pl.BlockSpec((L, LANES), lambda i: (i, vw // LANES)),
            pl.BlockSpec((1, LANES), lambda i: (0, 0)),
            pl.BlockSpec((1, vw), lambda i: (0, 0)),
        ],
        out_specs=pl.BlockSpec((L, vw), lambda i: (i, 0)),
        out_shape=jax.ShapeDtypeStruct((s, vw), BF16),
        scratch_shapes=[pltpu.VMEM((M_HEADS, M_QK, M_V + LANES), F32), pltpu.VMEM((SUBLANES, LANES), F32)],
        compiler_params=_params("arbitrary"),
        name="mlstm",
    )(qkv, qkv, qkv, og, og, gate_bias, norm_g)


def _rot_cols(w):
    half = QK_ROPE // 2
    return jnp.concatenate([-w[:, half:], w[:, :half]], axis=1)


def _place(w, start):
    return jnp.pad(w, ((0, 0), (start, HEAD_PAD - start - w.shape[1])))


def _block_diag(w):
    g, bi, bo = w.shape
    eye = jnp.eye(g, dtype=w.dtype)
    return (w[:, :, None, :] * eye[:, None, :, None]).reshape(g * bi, g * bo)


def kernel(x, positions, e_norm_g, e_w_in, e_lru_conv_w, e_lru_conv_b, e_lru_w_a, e_lru_b_a, e_lru_w_x,
           e_lru_b_x, e_lru_lambda, e_q_norm_g, e_w_qb, e_kv_norm_g, e_w_kvb, e_w_out, o_norm_g, o_w_in,
           o_b_igate, o_b_fgate, o_out_norm_g, o_w_out, f_norm_g, f_w_up, f_conv_w, f_conv_b, f_w_down,
           final_norm_g):
    bsz, s, d = x.shape
    assert bsz == 1 and d == D_MODEL
    h = x.reshape(s, d)
    pos = positions.reshape(s, 1)
    depth = f_norm_g.shape[0]

    half = QK_ROPE // 2
    inv_freq = ROPE_THETA ** (-jnp.arange(half, dtype=F32) / half)
    freq = _place(jnp.concatenate([inv_freq, inv_freq])[None, :], QK_NOPE)

    for layer in range(depth):
        j = layer // 2
        if layer % 2 == 0:
            w_in = e_w_in[j]
            c_kr = 2 * LRU_WIDTH + Q_LORA + KV_LORA
            w_kr = w_in[:, c_kr:c_kr + QK_ROPE]
            w_even = jnp.concatenate(
                [w_in[:, :c_kr], _place(w_kr, QK_NOPE), _place(_rot_cols(w_kr), QK_NOPE)], axis=1).astype(BF16)
            z, = _norm_matmul(h, e_norm_g[j], w_even, "even_in")

            y_lru = _rglru(
                z, e_lru_conv_w[j], e_lru_conv_b[j][None, :],
                _block_diag(e_lru_w_a[j]).astype(BF16), e_lru_b_a[j].reshape(1, LRU_WIDTH),
                _block_diag(e_lru_w_x[j]).astype(BF16), e_lru_b_x[j].reshape(1, LRU_WIDTH),
                e_lru_lambda[j][None, :])

            wq = e_w_qb[j].reshape(Q_LORA, MLA_HEADS, QK_NOPE + QK_ROPE)
            wq_a = jnp.pad(wq, ((0, 0), (0, 0), (0, HEAD_PAD - QK_NOPE - QK_ROPE)))
            wq_pe = wq[:, :, QK_NOPE:]
            wq_rot = jnp.concatenate([-wq_pe[:, :, half:], wq_pe[:, :, :half]], axis=2)
            wq_b = jnp.pad(wq_rot, ((0, 0), (0, 0), (QK_NOPE, HEAD_PAD - QK_NOPE - QK_ROPE)))
            wkv = e_w_kvb[j].reshape(KV_LORA, MLA_HEADS, QK_NOPE + V_HEAD)
            wk = jnp.pad(wkv[:, :, :QK_NOPE], ((0, 0), (0, 0), (0, HEAD_PAD - QK_NOPE)))
            hw = MLA_HEADS * HEAD_PAD
            qt, k, vt = _mla_prep(
                z, pos, e_q_norm_g[j][None, :], e_kv_norm_g[j][None, :],
                wq_a.reshape(Q_LORA, hw).T.astype(BF16), wq_b.reshape(Q_LORA, hw).T.astype(BF16),
                wk.reshape(KV_LORA, hw).astype(BF16),
                wkv[:, :, QK_NOPE:].reshape(KV_LORA, MLA_HEADS * V_HEAD).T.astype(BF16), freq)
            y_mla_t = _attention(qt, k, vt)

            w_out = e_w_out[j].astype(BF16)
            mix = ([y_lru, y_mla_t], [w_out[:LRU_WIDTH], w_out[LRU_WIDTH:]], (False, True))
        else:
            w_odd = jnp.pad(o_w_in[j], ((0, 0), (0, ODD_COLS - o_w_in.shape[2]))).astype(BF16)
            qkv, og = _norm_matmul(h, o_norm_g[j], w_odd, "odd_in", bf16_cols=ODD_QKV_COLS)
            gate_bias = jnp.pad(jnp.concatenate([o_b_igate[j], o_b_fgate[j]])[None, :],
                                ((0, 0), (0, LANES - 2 * M_HEADS)))
            y = _mlstm(qkv, og, gate_bias, o_out_norm_g[j][None, :])
            mix = ([y], [o_w_out[j].astype(BF16)], (False,))

        h = _ffn(h, *mix, f_norm_g[layer][None, :], f_w_up[layer].astype(BF16), f_conv_w[layer],
                 f_conv_b[layer][None, :], f_w_down[layer].astype(BF16), final_norm_g[None, :],
                 final_norm=(layer == depth - 1))
    return h.reshape(bsz, s, d)
```

```python
import collections
import functools

import jax
import jax.numpy as jnp
from jax import lax
from jax.experimental import pallas as pl
from jax.experimental.pallas import tpu as pltpu

F32 = jnp.float32
BF16 = jnp.bfloat16

EPS = 1e-6
LANES = 128
SUBLANES = 8
VMEM_LIMIT = 56 * 1024 * 1024

D_MODEL = 1024
LRU_WIDTH = 512
LRU_BLOCKS = 8
LRU_CONV = 4
LRU_C = 8.0
MLA_HEADS = 8
Q_LORA = 256
KV_LORA = 128
QK_NOPE = 64
QK_ROPE = 32
V_HEAD = 64
V_AUG = 80
ROPE_THETA = 10000.0
HEAD_PAD = 128
M_HEADS = 4
M_QK = 128
M_V = 256
GATE_CAP = 15.0
D_FF = 2816
FFN_CONV = 3

EVEN_COLS = 2 * LRU_WIDTH + Q_LORA + KV_LORA + 2 * HEAD_PAD
ODD_QKV_COLS = 2 * M_HEADS * M_QK + M_HEADS * M_V
ODD_COLS = ODD_QKV_COLS + M_HEADS * M_V + LANES

ROW_BLOCK = 512
ATTN_Q_BLOCK = 1024
ATTN_K_BLOCK = 256
ATTN_ROWS = 32
ATTN_HEADS = 2
LOG2E = 1.4426950408889634
M_CHUNK = 256
FFN_CHUNK = 256


def _params(*sem):
    return pltpu.CompilerParams(dimension_semantics=sem, vmem_limit_bytes=VMEM_LIMIT)


def _rms(x, g):
    inv = lax.rsqrt(jnp.mean(x * x, axis=-1, keepdims=True) + EPS)
    return x * inv * g


def _fold_rows(x, op):
    r = x.shape[0]
    while r > SUBLANES:
        r //= 2
        x = op(x[:r], x[r:])
    return x


def _shift_rows(x, prev_tail, sh):
    rolled = pltpu.roll(x, sh, axis=0)
    row = lax.broadcasted_iota(jnp.int32, prev_tail.shape, 0)
    head = jnp.where(row < sh, pltpu.roll(prev_tail, sh, axis=0), rolled[:SUBLANES])
    return jnp.concatenate([head, rolled[SUBLANES:]], axis=0)


def _norm_matmul_body(x_ref, g_ref, *refs):
    w_refs, o_refs = refs[:len(refs) // 2], refs[len(refs) // 2:]
    xn = _rms(x_ref[...], g_ref[...]).astype(BF16)
    for w_ref, o_ref in zip(w_refs, o_refs):
        o_ref[...] = jnp.dot(xn, w_ref[...], preferred_element_type=F32).astype(o_ref.dtype)


def _norm_matmul(x, g, ws, out_dtypes, name):
    s, d = x.shape
    tm = min(ROW_BLOCK, s)
    return pl.pallas_call(
        _norm_matmul_body,
        grid=(s // tm,),
        in_specs=[pl.BlockSpec((tm, d), lambda i: (i, 0)), pl.BlockSpec((1, d), lambda i: (0, 0))]
        + [pl.BlockSpec(w.shape, lambda i: (0, 0)) for w in ws],
        out_specs=[pl.BlockSpec((tm, w.shape[1]), lambda i: (i, 0)) for w in ws],
        out_shape=[jax.ShapeDtypeStruct((s, w.shape[1]), dt) for w, dt in zip(ws, out_dtypes)],
        compiler_params=_params("parallel"),
        name=name,
    )(x, g.reshape(1, d), *ws)


def _rglru_body(xr_ref, gate_ref, cw_ref, cb_ref, wa_ref, ba_ref, wx_ref, bx_ref, lam_ref,
                o_ref, x_s, h_s, a_s, b_s):
    @pl.when(pl.program_id(0) == 0)
    def _():
        x_s[:SUBLANES, :] = jnp.zeros((SUBLANES, x_s.shape[1]), F32)
        h_s[...] = jnp.zeros_like(h_s)

    tm = xr_ref.shape[0]
    xr = xr_ref[...]
    x_s[SUBLANES:, :] = xr
    xc = cb_ref[...] + x_s[SUBLANES - 3:SUBLANES - 3 + tm, :] * cw_ref[0:1, :]
    xc = xc + x_s[SUBLANES - 2:SUBLANES - 2 + tm, :] * cw_ref[1:2, :]
    xc = xc + x_s[SUBLANES - 1:SUBLANES - 1 + tm, :] * cw_ref[2:3, :]
    xc = xc + xr * cw_ref[3:4, :]
    x_s[:SUBLANES, :] = xr[tm - SUBLANES:, :]

    xcb = xc.astype(BF16)
    r = jax.nn.sigmoid(jnp.dot(xcb, wa_ref[...], preferred_element_type=F32) + ba_ref[...])
    i = jax.nn.sigmoid(jnp.dot(xcb, wx_ref[...], preferred_element_type=F32) + bx_ref[...])
    neg_lam = -lam_ref[...]
    softplus = jnp.maximum(neg_lam, 0.0) + jnp.log1p(jnp.exp(-jnp.abs(neg_lam)))
    log_a = (-LRU_C * r) * softplus
    a = jnp.exp(log_a)
    th = jnp.tanh(log_a)
    u = jnp.sqrt(-2.0 * th) * lax.rsqrt(1.0 - th) * (i * xc)

    w = a.shape[1]
    a = a.reshape(tm // SUBLANES, SUBLANES, w)
    u = u.reshape(tm // SUBLANES, SUBLANES, w)
    sub = lax.broadcasted_iota(jnp.int32, a.shape, 1)
    for d in (1, 2, 4):
        keep = sub >= d
        a_sh = jnp.where(keep, pltpu.roll(a, d, axis=1), 1.0)
        u_sh = jnp.where(keep, pltpu.roll(u, d, axis=1), 0.0)
        u = a * u_sh + u
        a = a * a_sh
    a_s[...] = a.reshape(tm, w)
    b_s[...] = u.reshape(tm, w)

    def group(k, h):
        off = pl.multiple_of(k * SUBLANES, SUBLANES)
        ht = a_s[pl.ds(off, SUBLANES), :] * h + b_s[pl.ds(off, SUBLANES), :]
        b_s[pl.ds(off, SUBLANES), :] = ht
        return ht[SUBLANES - 1:, :]

    h_s[...] = lax.fori_loop(0, tm // SUBLANES, group, h_s[...], unroll=8)
    o_ref[...] = (b_s[...] * jax.nn.gelu(gate_ref[...])).astype(o_ref.dtype)


def _rglru(z, cw, cb, wa, ba, wx, bx, lam):
    s = z.shape[0]
    w = LRU_WIDTH
    tm = min(ROW_BLOCK, s)
    vec = lambda r: pl.BlockSpec((r, w), lambda i: (0, 0))
    return pl.pallas_call(
        _rglru_body,
        grid=(s // tm,),
        in_specs=[
            pl.BlockSpec((tm, w), lambda i: (i, 0)),
            pl.BlockSpec((tm, w), lambda i: (i, 1)),
            vec(LRU_CONV), vec(1), vec(w), vec(1), vec(w), vec(1), vec(1),
        ],
        out_specs=pl.BlockSpec((tm, w), lambda i: (i, 0)),
        out_shape=jax.ShapeDtypeStruct((s, w), BF16),
        scratch_shapes=[
            pltpu.VMEM((SUBLANES + tm, w), F32),
            pltpu.VMEM((1, w), F32),
            pltpu.VMEM((tm, w), F32),
            pltpu.VMEM((tm, w), F32),
        ],
        compiler_params=_params("arbitrary"),
        name="rglru",
    )(z, z, cw, cb, wa, ba, wx, bx, lam)


def _mla_prep_body(zq_ref, zkv_ref, zkr_ref, zkrot_ref, pos_ref, qg_ref, kvg_ref,
                   wqa_ref, wqb_ref, wk_ref, wv_ref, freq_ref, qt_ref, k_ref, vt_ref):
    nt = (((1,), (1,)), ((), ()))
    ang_t = freq_ref[...] * pos_ref[...].astype(F32)
    cos_t = jnp.cos(ang_t)
    sin_t = jnp.sin(ang_t)
    row = lax.broadcasted_iota(jnp.int32, (HEAD_PAD, 1), 0)
    scale = (QK_NOPE + QK_ROPE) ** -0.5 * LOG2E
    cq_t = jnp.where(row < QK_NOPE + QK_ROPE, cos_t, 0.0) * scale
    sq_t = sin_t * scale
    ck = jnp.where(row >= QK_NOPE, cos_t, 0.0).T
    sinv = sin_t.T

    qn = _rms(zq_ref[...], qg_ref[...]).astype(BF16)
    qa_t = lax.dot_general(wqa_ref[...], qn, nt, preferred_element_type=F32)
    qb_t = lax.dot_general(wqb_ref[...], qn, nt, preferred_element_type=F32)
    kvn = _rms(zkv_ref[...], kvg_ref[...]).astype(BF16)
    kn = jnp.dot(kvn, wk_ref[...], preferred_element_type=F32)
    v_t = lax.dot_general(wv_ref[...], kvn, nt, preferred_element_type=F32).astype(BF16)
    tk = vt_ref.shape[2]
    ones = jnp.ones((V_AUG - V_HEAD, tk), BF16)
    for c in range(vt_ref.shape[0]):
        for h in range(MLA_HEADS):
            vt_ref[c, h * V_AUG:h * V_AUG + V_HEAD, :] = v_t[h * V_HEAD:(h + 1) * V_HEAD, c * tk:(c + 1) * tk]
            vt_ref[c, h * V_AUG + V_HEAD:(h + 1) * V_AUG, :] = ones
    kpe = zkr_ref[...] * ck + zkrot_ref[...] * sinv
    for h in range(MLA_HEADS):
        hs = slice(h * HEAD_PAD, (h + 1) * HEAD_PAD)
        qt_ref[0, hs, :] = (qa_t[hs, :] * cq_t + qb_t[hs, :] * sq_t).astype(BF16)
        k_ref[:, hs] = (kn[:, hs] + kpe).astype(BF16)


def _mla_prep(z, zkr, pos, qg, kvg, wqa_t, wqb_t, wk, wv_t, freq):
    s = z.shape[0]
    tm = min(ATTN_Q_BLOCK, s)
    tk = ATTN_K_BLOCK
    assert s % tm == 0 and tm % (2 * tk) == 0
    nb = s // tm
    hw = MLA_HEADS * HEAD_PAD
    vw = MLA_HEADS * V_AUG
    c0 = 2 * LRU_WIDTH
    full = lambda a: pl.BlockSpec(a.shape, lambda i: (0, 0))
    return pl.pallas_call(
        _mla_prep_body,
        grid=(nb,),
        in_specs=[
            pl.BlockSpec((tm, Q_LORA), lambda i: (i, c0 // Q_LORA)),
            pl.BlockSpec((tm, KV_LORA), lambda i: (i, (c0 + Q_LORA) // KV_LORA)),
            pl.BlockSpec((tm, HEAD_PAD), lambda i: (i, 0)),
            pl.BlockSpec((tm, HEAD_PAD), lambda i: (i, 1)),
            pl.BlockSpec((1, tm), lambda i: (0, i)),
            full(qg), full(kvg), full(wqa_t), full(wqb_t), full(wk), full(wv_t), full(freq),
        ],
        out_specs=[
            pl.BlockSpec((1, hw, tm), lambda i: (i, 0, 0)),
            pl.BlockSpec((tm, hw), lambda i: (i, 0)),
            pl.BlockSpec((tm // tk, vw, tk), lambda i: (i, 0, 0)),
        ],
        out_shape=[
            jax.ShapeDtypeStruct((nb, hw, tm), BF16),
            jax.ShapeDtypeStruct((s, hw), BF16),
            jax.ShapeDtypeStruct((s // tk, vw, tk), BF16),
        ],
        compiler_params=_params("parallel"),
        name="mla_prep",
    )(z, z, zkr, zkr, pos, qg, kvg, wqa_t, wqb_t, wk, wv_t, freq)


_AttnHead = collections.namedtuple("_AttnHead", "s_buf smax_buf p_buf acc qt kcols vrows orows")


def _attn_body(qt_ref, k_ref, vt_ref, o_ref, *scratch):
    ns = ATTN_Q_BLOCK // ATTN_K_BLOCK
    ahead = ns - 2
    per_head = 2 * ns + 3
    tq = qt_ref.shape[2]
    tk = scratch[0].shape[0]
    i = pl.program_id(1)
    neg = jnp.finfo(F32).min

    def head(hh):
        sc = scratch[hh * per_head:(hh + 1) * per_head]
        return _AttnHead(
            s_buf=sc[:ns], smax_buf=sc[ns:2 * ns], p_buf=sc[2 * ns:2 * ns + 2], acc=sc[-1],
            qt=qt_ref[0, hh * HEAD_PAD:(hh + 1) * HEAD_PAD, :],
            kcols=slice(hh * HEAD_PAD, (hh + 1) * HEAD_PAD), vrows=slice(hh * V_AUG, (hh + 1) * V_AUG),
            orows=slice(hh * V_HEAD, (hh + 1) * V_HEAD))

    heads = [head(hh) for hh in range(ATTN_HEADS)]

    def qk(hd, b, c, lane0=0):
        off = pl.multiple_of(b * tk, tk)
        s = jnp.dot(k_ref[pl.ds(off, tk), hd.kcols], hd.qt[:, lane0:], preferred_element_type=F32)
        hd.s_buf[c % ns][:, lane0:] = s
        hd.smax_buf[c % ns][:, lane0:] = _fold_rows(s, jnp.maximum)

    def softmax(hd, c, m, key_off=None, lane0=0):
        sb = hd.s_buf[c % ns]
        p_buf = hd.p_buf[c % 2]
        chunks = range(0, tk, ATTN_ROWS)

        def load(r):
            s = sb[r:r + ATTN_ROWS, lane0:]
            if key_off is None:
                return s
            key = lax.broadcasted_iota(jnp.int32, s.shape, 0) + (key_off + r)
            qry = lax.broadcasted_iota(jnp.int32, s.shape, 1) + lane0
            return jnp.where(key <= qry, s, neg)

        if key_off is None:
            smax = hd.smax_buf[c % ns][...]
        else:
            smax = functools.reduce(jnp.maximum, [_fold_rows(load(r), jnp.maximum) for r in chunks])
        m_old = m[:, lane0:]
        m_new = jnp.maximum(m_old, jnp.max(smax, axis=0, keepdims=True))
        for r in chunks:
            p_buf[r:r + ATTN_ROWS, lane0:] = jnp.exp2(load(r) - m_new).astype(BF16)
        alpha = jnp.exp2(m_old - m_new)
        if lane0:
            p_buf[:, :lane0] = jnp.zeros((tk, lane0), BF16)
            m_new = jnp.concatenate([m[:, :lane0], m_new], axis=1)
            alpha = jnp.concatenate([jnp.ones((1, lane0), F32), alpha], axis=1)
        return m_new, alpha

    def pv(hd, b, c, alpha):
        hd.acc[...] = alpha * hd.acc[...] + jnp.dot(vt_ref[b, hd.vrows, :], hd.p_buf[c % 2][...],
                                                    preferred_element_type=F32)

    def trip(t, carry):
        carry = list(carry)
        for c in range(ns):
            b = t * ns + c
            for hh, hd in enumerate(heads):
                m, alpha = carry[hh]
                qk(hd, b + ahead, c + ahead)
                pv(hd, jnp.maximum(b - 1, 0), c - 1, alpha)
                carry[hh] = softmax(hd, c, m)
        return tuple(carry)

    for hd in heads:
        for c in range(ahead):
            qk(hd, c, c)
        hd.p_buf[1][...] = jnp.zeros_like(hd.p_buf[1])
        hd.acc[...] = jnp.zeros_like(hd.acc)
    init = (jnp.full((1, tq), -jnp.inf, F32), jnp.ones((1, tq), F32))
    carry = list(lax.fori_loop(0, i, trip, (init,) * ATTN_HEADS))

    b0 = i * ns
    for c in range(ns):
        for hh, hd in enumerate(heads):
            m, alpha = carry[hh]
            if c + ahead < ns:
                qk(hd, b0 + c + ahead, c + ahead, (c + ahead) * tk)
            pv(hd, jnp.maximum(b0 + c - 1, 0), c - 1, alpha)
            carry[hh] = softmax(hd, c, m, c * tk, c * tk)
    for hh, hd in enumerate(heads):
        pv(hd, b0 + ns - 1, ns - 1, carry[hh][1])
        o_ref[hd.orows, :] = (hd.acc[:V_HEAD, :] / hd.acc[V_HEAD:V_HEAD + 1, :]).astype(o_ref.dtype)


def _attention(qt, k, vt):
    nb, hw, tq = qt.shape
    nkb, vw, tk = vt.shape
    s = k.shape[0]
    hp = ATTN_HEADS
    per_head = ([pltpu.VMEM((tk, tq), F32)] * (tq // tk) + [pltpu.VMEM((SUBLANES, tq), F32)] * (tq // tk)
                + [pltpu.VMEM((tk, tq), BF16)] * 2 + [pltpu.VMEM((V_AUG, tq), F32)])
    return pl.pallas_call(
        _attn_body,
        grid=(MLA_HEADS // hp, nb),
        in_specs=[
            pl.BlockSpec((1, hp * HEAD_PAD, tq), lambda g, i: (i, g, 0)),
            pl.BlockSpec((s, hp * HEAD_PAD), lambda g, i: (0, g)),
            pl.BlockSpec((nkb, hp * V_AUG, tk), lambda g, i: (0, g, 0)),
        ],
        out_specs=pl.BlockSpec((hp * V_HEAD, tq), lambda g, i: (g, i)),
        out_shape=jax.ShapeDtypeStruct((MLA_HEADS * V_HEAD, s), BF16),
        scratch_shapes=per_head * hp,
        compiler_params=_params("parallel", "parallel"),
        name="attention",
    )(qt, k, vt)


def _ffn_body(*refs, transposed, final_norm):
    n_mix = len(transposed)
    res_ref = refs[0]
    y_refs, w_refs = refs[1:1 + n_mix], refs[1 + n_mix:1 + 2 * n_mix]
    g_ref, wup_ref, cw_ref, cb_ref, wdn_ref, fg_ref, o_ref, tail_s, act_s = refs[1 + 2 * n_mix:]

    @pl.when(pl.program_id(0) == 0)
    def _():
        tail_s[...] = jnp.zeros_like(tail_s)

    tm = res_ref.shape[0]
    x = res_ref[...]
    for y_ref, w_ref, tr in zip(y_refs, w_refs, transposed):
        dims = (((0,), (0,)), ((), ())) if tr else (((1,), (0,)), ((), ()))
        x = x + lax.dot_general(y_ref[...], w_ref[...], dims, preferred_element_type=F32)
    xn = _rms(x, g_ref[...]).astype(BF16)

    def conv(col):
        cs = pl.ds(col, FFN_CHUNK)
        u = jnp.dot(xn, wup_ref[:, cs], preferred_element_type=F32)
        tail = tail_s[:, cs]
        tail_s[:, cs] = u[tm - SUBLANES:, :]
        y = cb_ref[:, cs] + _shift_rows(u, tail, 2) * cw_ref[0:1, cs]
        y = y + _shift_rows(u, tail, 1) * cw_ref[1:2, cs]
        return y + u * cw_ref[2:3, cs]

    for c in range(D_FF // FFN_CHUNK):
        gate = conv(c * FFN_CHUNK)
        val = conv(D_FF + c * FFN_CHUNK)
        act_s[:, pl.ds(c * FFN_CHUNK, FFN_CHUNK)] = (jax.nn.silu(gate) * val).astype(BF16)

    out = x + jnp.dot(act_s[...], wdn_ref[...], preferred_element_type=F32)
    if final_norm:
        out = _rms(out, fg_ref[...])
    o_ref[...] = out


def _ffn(res, ys, ws, transposed, layer, g, wup, cw, cb, wdn, fg, final_norm):
    s, d = res.shape
    tm = min(ROW_BLOCK, s)
    const = lambda a: pl.BlockSpec(a.shape, lambda i: (0, 0), pipeline_mode=pl.Buffered(1))
    stacked = lambda a: pl.BlockSpec((None,) + a.shape[1:], lambda i: (layer, 0, 0), pipeline_mode=pl.Buffered(1))
    y_specs = [pl.BlockSpec((y.shape[0], tm), lambda i: (0, i)) if tr else
               pl.BlockSpec((tm, y.shape[1]), lambda i: (i, 0)) for y, tr in zip(ys, transposed)]
    return pl.pallas_call(
        functools.partial(_ffn_body, transposed=transposed, final_norm=final_norm),
        grid=(s // tm,),
        in_specs=[pl.BlockSpec((tm, d), lambda i: (i, 0))] + y_specs + [const(w) for w in ws]
        + [stacked(g), stacked(wup), stacked(cw), stacked(cb), stacked(wdn), const(fg)],
        out_specs=pl.BlockSpec((tm, d), lambda i: (i, 0)),
        out_shape=jax.ShapeDtypeStruct((s, d), F32),
        scratch_shapes=[pltpu.VMEM((SUBLANES, 2 * D_FF), F32), pltpu.VMEM((tm, D_FF), BF16)],
        compiler_params=_params("arbitrary"),
        name="ffn_final" if final_norm else "ffn",
    )(res, *ys, *ws, g, wup, cw, cb, wdn, fg)


def _mlstm_body(q_ref, k_ref, v_ref, o_ref, gate_ref, gb_ref, ng_ref, y_ref, c_s, m_s):
    @pl.when(pl.program_id(0) == 0)
    def _():
        c_s[...] = jnp.zeros_like(c_s)
        m_s[...] = jnp.zeros_like(m_s)

    L = q_ref.shape[0]
    capped = GATE_CAP * jnp.tanh((gate_ref[...] + gb_ref[...]) / GATE_CAP)
    log_f = jax.nn.log_sigmoid(capped)
    row = lax.broadcasted_iota(jnp.int32, (L, LANES), 0)
    b_all = log_f
    d = 1
    while d < L:
        b_all = b_all + jnp.where(row >= d, pltpu.roll(b_all, d, axis=0), 0.0)
        d *= 2
    i_t = capped.T
    b_t = b_all.T

    tri = lax.broadcasted_iota(jnp.int32, (L, L), 1) <= lax.broadcasted_iota(jnp.int32, (L, L), 0)
    ones = jnp.ones((L, LANES), BF16)
    q_scale = M_QK ** -0.5
    for h in range(M_HEADS):
        b_col = b_all[:, M_HEADS + h:M_HEADS + h + 1]
        i_col = capped[:, h:h + 1]
        b_row = b_t[M_HEADS + h:M_HEADS + h + 1, :]
        i_row = i_t[h:h + 1, :]
        m_st = m_s[h:h + 1, 0:1]
        c_st = c_s[h]

        dmat = jnp.where(tri, b_col - b_row + i_row, -jnp.inf)
        inter = b_col + m_st
        m_t = jnp.maximum(inter, jnp.max(dmat, axis=-1, keepdims=True))
        w = jnp.exp(dmat - m_t)
        g = jnp.exp(inter - m_t)

        qh = q_ref[:, h * M_QK:(h + 1) * M_QK]
        kh = k_ref[:, h * M_QK:(h + 1) * M_QK]
        vaug = jnp.concatenate([v_ref[:, h * M_V:(h + 1) * M_V], ones], axis=1)
        sc = lax.dot_general(qh, kh, (((1,), (1,)), ((), ())), preferred_element_type=F32) * (w * q_scale)
        num_aug = (g * q_scale) * jnp.dot(qh, c_st.astype(BF16), preferred_element_type=F32)
        num_aug = num_aug + jnp.dot(sc.astype(BF16), vaug, preferred_element_type=F32)
        den = num_aug[:, M_V:M_V + 1]
        hh = num_aug[:, :M_V] / jnp.maximum(jnp.abs(den), jnp.exp(-m_t))

        b_last = b_col[L - 1:, :]
        w_end = b_last - b_col + i_col
        m_new = jnp.maximum(b_last + m_st, jnp.max(w_end, axis=0, keepdims=True))
        g_end = jnp.exp(b_last + m_st - m_new)
        kw = (kh.astype(F32) * jnp.exp(w_end - m_new)).astype(BF16)
        c_s[h] = g_end * c_st + lax.dot_general(kw, vaug, (((0,), (0,)), ((), ())),
                                                preferred_element_type=F32)
        m_s[h:h + 1, :] = jnp.broadcast_to(m_new, (1, LANES))

        hn = hh * lax.rsqrt(jnp.mean(hh * hh, axis=-1, keepdims=True) + EPS)
        vs = slice(h * M_V, (h + 1) * M_V)
        y_ref[:, vs] = (hn * ng_ref[:, vs] * jax.nn.sigmoid(o_ref[:, vs])).astype(y_ref.dtype)


def _mlstm(qkv, og, gate_bias, norm_g):
    s = qkv.shape[0]
    L = min(M_CHUNK, s)
    qk = M_HEADS * M_QK
    vw = M_HEADS * M_V
    return pl.pallas_call(
        _mlstm_body,
        grid=(s // L,),
        in_specs=[
            pl.BlockSpec((L, qk), lambda i: (i, 0)),
            pl.BlockSpec((L, qk), lambda i: (i, 1)),
            pl.BlockSpec((L, vw), lambda i: (i, 2 * qk // vw)),
            pl.BlockSpec((L, vw), lambda i: (i, 0)),
            pl.BlockSpec((L, LANES), lambda i: (i, vw // LANES)),
            pl.BlockSpec((1, LANES), lambda i: (0, 0)),
            pl.BlockSpec((1, vw), lambda i: (0, 0)),
        ],
        out_specs=pl.BlockSpec((L, vw), lambda i: (i, 0)),
        out_shape=jax.ShapeDtypeStruct((s, vw), BF16),
        scratch_shapes=[pltpu.VMEM((M_HEADS, M_QK, M_V + LANES), F32), pltpu.VMEM((SUBLANES, LANES), F32)],
        compiler_params=_params("arbitrary"),
        name="mlstm",
    )(qkv, qkv, qkv, og, og, gate_bias, norm_g)


def _rot_cols(w):
    half = QK_ROPE // 2
    return jnp.concatenate([-w[:, half:], w[:, :half]], axis=1)


def _place(w, start):
    return jnp.pad(w, ((0, 0), (start, HEAD_PAD - start - w.shape[1])))


def _block_diag(w):
    g, bi, bo = w.shape
    eye = jnp.eye(g, dtype=w.dtype)
    return (w[:, :, None, :] * eye[:, None, :, None]).reshape(g * bi, g * bo)


def kernel(x, positions, e_norm_g, e_w_in, e_lru_conv_w, e_lru_conv_b, e_lru_w_a, e_lru_b_a, e_lru_w_x,
           e_lru_b_x, e_lru_lambda, e_q_norm_g, e_w_qb, e_kv_norm_g, e_w_kvb, e_w_out, o_norm_g, o_w_in,
           o_b_igate, o_b_fgate, o_out_norm_g, o_w_out, f_norm_g, f_w_up, f_conv_w, f_conv_b, f_w_down,
           final_norm_g):
    bsz, s, d = x.shape
    assert bsz == 1 and d == D_MODEL
    h = x.reshape(s, d)
    pos = positions.reshape(1, s)
    depth = f_norm_g.shape[0]

    half = QK_ROPE // 2
    inv_freq = ROPE_THETA ** (-jnp.arange(half, dtype=F32) / half)
    freq = _place(jnp.concatenate([inv_freq, inv_freq])[None, :], QK_NOPE).reshape(HEAD_PAD, 1)
    f_w_up_b = f_w_up.astype(BF16)
    f_w_down_b = f_w_down.astype(BF16)

    for layer in range(depth):
        j = layer // 2
        if layer % 2 == 0:
            w_in = e_w_in[j]
            c_kr = 2 * LRU_WIDTH + Q_LORA + KV_LORA
            w_kr = w_in[:, c_kr:c_kr + QK_ROPE]
            w_kr_placed = jnp.concatenate([_place(w_kr, QK_NOPE), _place(_rot_cols(w_kr), QK_NOPE)], axis=1)
            z, zkr = _norm_matmul(h, e_norm_g[j], [w_in[:, :c_kr].astype(BF16), w_kr_placed.astype(BF16)],
                                  (F32, F32), "even_in")

            y_lru = _rglru(
                z, e_lru_conv_w[j], e_lru_conv_b[j][None, :],
                _block_diag(e_lru_w_a[j]).astype(BF16), e_lru_b_a[j].reshape(1, LRU_WIDTH),
                _block_diag(e_lru_w_x[j]).astype(BF16), e_lru_b_x[j].reshape(1, LRU_WIDTH),
                e_lru_lambda[j][None, :])

            wq = e_w_qb[j].reshape(Q_LORA, MLA_HEADS, QK_NOPE + QK_ROPE)
            wq_a = jnp.pad(wq, ((0, 0), (0, 0), (0, HEAD_PAD - QK_NOPE - QK_ROPE)))
            wq_pe = wq[:, :, QK_NOPE:]
            wq_rot = jnp.concatenate([-wq_pe[:, :, half:], wq_pe[:, :, :half]], axis=2)
            wq_b = jnp.pad(wq_rot, ((0, 0), (0, 0), (QK_NOPE, HEAD_PAD - QK_NOPE - QK_ROPE)))
            wkv = e_w_kvb[j].reshape(KV_LORA, MLA_HEADS, QK_NOPE + V_HEAD)
            wk = jnp.pad(wkv[:, :, :QK_NOPE], ((0, 0), (0, 0), (0, HEAD_PAD - QK_NOPE)))
            hw = MLA_HEADS * HEAD_PAD
            qt, k, vt = _mla_prep(
                z, zkr, pos, e_q_norm_g[j][None, :], e_kv_norm_g[j][None, :],
                wq_a.reshape(Q_LORA, hw).T.astype(BF16), wq_b.reshape(Q_LORA, hw).T.astype(BF16),
                wk.reshape(KV_LORA, hw).astype(BF16),
                wkv[:, :, QK_NOPE:].reshape(KV_LORA, MLA_HEADS * V_HEAD).T.astype(BF16), freq)
            y_mla_t = _attention(qt, k, vt)

            w_out = e_w_out[j].astype(BF16)
            mix = ([y_lru, y_mla_t], [w_out[:LRU_WIDTH], w_out[LRU_WIDTH:]], (False, True))
        else:
            w_in = o_w_in[j]
            w_og = jnp.pad(w_in[:, ODD_QKV_COLS:], ((0, 0), (0, ODD_COLS - w_in.shape[1])))
            qkv, og = _norm_matmul(h, o_norm_g[j], [w_in[:, :ODD_QKV_COLS].astype(BF16), w_og.astype(BF16)],
                                   (BF16, F32), "odd_in")
            gate_bias = jnp.pad(jnp.concatenate([o_b_igate[j], o_b_fgate[j]])[None, :],
                                ((0, 0), (0, LANES - 2 * M_HEADS)))
            y = _mlstm(qkv, og, gate_bias, o_out_norm_g[j][None, :])
            mix = ([y], [o_w_out[j].astype(BF16)], (False,))

        h = _ffn(h, *mix, layer, f_norm_g[:, None, :], f_w_up_b, f_conv_w, f_conv_b[:, None, :], f_w_down_b,
                 final_norm_g[None, :], final_norm=(layer == depth - 1))
    return h.reshape(bsz, s, d)
```

```python
import collections
import functools

import jax
import jax.numpy as jnp
from jax import lax
from jax.experimental import pallas as pl
from jax.experimental.pallas import tpu as pltpu

F32 = jnp.float32
BF16 = jnp.bfloat16

EPS = 1e-6
LANES = 128
SUBLANES = 8
VMEM_LIMIT = 56 * 1024 * 1024

D_MODEL = 1024
LRU_WIDTH = 512
LRU_BLOCKS = 8
LRU_CONV = 4
LRU_C = 8.0
MLA_HEADS = 8
Q_LORA = 256
KV_LORA = 128
QK_NOPE = 64
QK_ROPE = 32
V_HEAD = 64
V_AUG = 80
ROPE_THETA = 10000.0
HEAD_PAD = 128
M_HEADS = 4
M_QK = 128
M_V = 256
GATE_CAP = 15.0
D_FF = 2816
FFN_CONV = 3

EVEN_COLS = 2 * LRU_WIDTH + Q_LORA + KV_LORA + 2 * HEAD_PAD
ODD_QKV_COLS = 2 * M_HEADS * M_QK + M_HEADS * M_V
ODD_COLS = ODD_QKV_COLS + M_HEADS * M_V + LANES

ROW_BLOCK = 512
ATTN_Q_BLOCK = 1024
ATTN_K_BLOCK = 256
ATTN_ROWS = 32
ATTN_HEADS = 2
LOG2E = 1.4426950408889634
M_CHUNK = 256
FFN_CHUNK = 256


def _params(*sem):
    return pltpu.CompilerParams(dimension_semantics=sem, vmem_limit_bytes=VMEM_LIMIT)


def _rms(x, g):
    inv = lax.rsqrt(jnp.mean(x * x, axis=-1, keepdims=True) + EPS)
    return x * inv * g


def _fold_rows(x, op):
    r = x.shape[0]
    while r > SUBLANES:
        r //= 2
        x = op(x[:r], x[r:])
    return x


def _shift_rows(x, prev_tail, sh):
    rolled = pltpu.roll(x, sh, axis=0)
    row = lax.broadcasted_iota(jnp.int32, prev_tail.shape, 0)
    head = jnp.where(row < sh, pltpu.roll(prev_tail, sh, axis=0), rolled[:SUBLANES])
    return jnp.concatenate([head, rolled[SUBLANES:]], axis=0)


def _norm_matmul_body(x_ref, g_ref, *refs):
    w_refs, o_refs = refs[:len(refs) // 2], refs[len(refs) // 2:]
    xn = _rms(x_ref[...], g_ref[...]).astype(BF16)
    for w_ref, o_ref in zip(w_refs, o_refs):
        o_ref[...] = jnp.dot(xn, w_ref[...], preferred_element_type=F32).astype(o_ref.dtype)


def _norm_matmul(x, g, ws, out_dtypes, name):
    s, d = x.shape
    tm = min(ROW_BLOCK, s)
    return pl.pallas_call(
        _norm_matmul_body,
        grid=(s // tm,),
        in_specs=[pl.BlockSpec((tm, d), lambda i: (i, 0)), pl.BlockSpec((1, d), lambda i: (0, 0))]
        + [pl.BlockSpec(w.shape, lambda i: (0, 0)) for w in ws],
        out_specs=[pl.BlockSpec((tm, w.shape[1]), lambda i: (i, 0)) for w in ws],
        out_shape=[jax.ShapeDtypeStruct((s, w.shape[1]), dt) for w, dt in zip(ws, out_dtypes)],
        compiler_params=_params("parallel"),
        name=name,
    )(x, g.reshape(1, d), *ws)


def _rglru_body(xr_ref, gate_ref, cw_ref, cb_ref, wa_ref, ba_ref, wx_ref, bx_ref, lam_ref,
                o_ref, x_s, h_s, a_s, b_s):
    @pl.when(pl.program_id(0) == 0)
    def _():
        x_s[:SUBLANES, :] = jnp.zeros((SUBLANES, x_s.shape[1]), F32)
        h_s[...] = jnp.zeros_like(h_s)

    tm = xr_ref.shape[0]
    xr = xr_ref[...]
    x_s[SUBLANES:, :] = xr
    xc = cb_ref[...] + x_s[SUBLANES - 3:SUBLANES - 3 + tm, :] * cw_ref[0:1, :]
    xc = xc + x_s[SUBLANES - 2:SUBLANES - 2 + tm, :] * cw_ref[1:2, :]
    xc = xc + x_s[SUBLANES - 1:SUBLANES - 1 + tm, :] * cw_ref[2:3, :]
    xc = xc + xr * cw_ref[3:4, :]
    x_s[:SUBLANES, :] = xr[tm - SUBLANES:, :]

    xcb = xc.astype(BF16)
    r = jax.nn.sigmoid(jnp.dot(xcb, wa_ref[...], preferred_element_type=F32) + ba_ref[...])
    i = jax.nn.sigmoid(jnp.dot(xcb, wx_ref[...], preferred_element_type=F32) + bx_ref[...])
    neg_lam = -lam_ref[...]
    softplus = jnp.maximum(neg_lam, 0.0) + jnp.log1p(jnp.exp(-jnp.abs(neg_lam)))
    log_a = (-LRU_C * r) * softplus
    a = jnp.exp(log_a)
    th = jnp.tanh(log_a)
    u = jnp.sqrt(-2.0 * th) * lax.rsqrt(1.0 - th) * (i * xc)

    w = a.shape[1]
    a = a.reshape(tm // SUBLANES, SUBLANES, w)
    u = u.reshape(tm // SUBLANES, SUBLANES, w)
    sub = lax.broadcasted_iota(jnp.int32, a.shape, 1)
    for d in (1, 2, 4):
        keep = sub >= d
        a_sh = jnp.where(keep, pltpu.roll(a, d, axis=1), 1.0)
        u_sh = jnp.where(keep, pltpu.roll(u, d, axis=1), 0.0)
        u = a * u_sh + u
        a = a * a_sh
    a_s[...] = a.reshape(tm, w)
    b_s[...] = u.reshape(tm, w)

    def group(k, h):
        off = pl.multiple_of(k * SUBLANES, SUBLANES)
        ht = a_s[pl.ds(off, SUBLANES), :] * h + b_s[pl.ds(off, SUBLANES), :]
        b_s[pl.ds(off, SUBLANES), :] = ht
        return ht[SUBLANES - 1:, :]

    h_s[...] = lax.fori_loop(0, tm // SUBLANES, group, h_s[...], unroll=8)
    o_ref[...] = (b_s[...] * jax.nn.gelu(gate_ref[...])).astype(o_ref.dtype)


def _rglru(z, cw, cb, wa, ba, wx, bx, lam):
    s = z.shape[0]
    w = LRU_WIDTH
    tm = min(ROW_BLOCK, s)
    vec = lambda r: pl.BlockSpec((r, w), lambda i: (0, 0))
    return pl.pallas_call(
        _rglru_body,
        grid=(s // tm,),
        in_specs=[
            pl.BlockSpec((tm, w), lambda i: (i, 0)),
            pl.BlockSpec((tm, w), lambda i: (i, 1)),
            vec(LRU_CONV), vec(1), vec(w), vec(1), vec(w), vec(1), vec(1),
        ],
        out_specs=pl.BlockSpec((tm, w), lambda i: (i, 0)),
        out_shape=jax.ShapeDtypeStruct((s, w), BF16),
        scratch_shapes=[
            pltpu.VMEM((SUBLANES + tm, w), F32),
            pltpu.VMEM((1, w), F32),
            pltpu.VMEM((tm, w), F32),
            pltpu.VMEM((tm, w), F32),
        ],
        compiler_params=_params("arbitrary"),
        name="rglru",
    )(z, z, cw, cb, wa, ba, wx, bx, lam)


def _mla_prep_body(zq_ref, zkv_ref, zkr_ref, zkrot_ref, pos_ref, qg_ref, kvg_ref,
                   wqa_ref, wqb_ref, wk_ref, wv_ref, freq_ref, qt_ref, k_ref, vt_ref):
    nt = (((1,), (1,)), ((), ()))
    tm = pos_ref.shape[1]
    ang = freq_ref[...] * pos_ref[...].astype(F32)
    cos_h, sin_h = jnp.cos(ang), jnp.sin(ang)
    zeros = lambda r: jnp.zeros((r, tm), F32)
    cos_t = jnp.concatenate([jnp.ones((QK_NOPE, tm), F32), cos_h, cos_h, zeros(HEAD_PAD - QK_NOPE - QK_ROPE)], axis=0)
    sin_t = jnp.concatenate([zeros(QK_NOPE), sin_h, sin_h, zeros(HEAD_PAD - QK_NOPE - QK_ROPE)], axis=0)
    scale = (QK_NOPE + QK_ROPE) ** -0.5 * LOG2E
    cq_t = cos_t * scale
    sq_t = sin_t * scale
    ck = cos_t.T
    sinv = sin_t.T

    qn = _rms(zq_ref[...], qg_ref[...]).astype(BF16)
    qa_t = lax.dot_general(wqa_ref[...], qn, nt, preferred_element_type=F32)
    qb_t = lax.dot_general(wqb_ref[...], qn, nt, preferred_element_type=F32)
    kvn = _rms(zkv_ref[...], kvg_ref[...]).astype(BF16)
    kn = jnp.dot(kvn, wk_ref[...], preferred_element_type=F32)
    v_t = lax.dot_general(wv_ref[...], kvn, nt, preferred_element_type=F32).astype(BF16)
    tk = vt_ref.shape[2]
    ones = jnp.ones((V_AUG - V_HEAD, tk), BF16)
    for c in range(vt_ref.shape[0]):
        for h in range(MLA_HEADS):
            vt_ref[c, h * V_AUG:h * V_AUG + V_HEAD, :] = v_t[h * V_HEAD:(h + 1) * V_HEAD, c * tk:(c + 1) * tk]
            vt_ref[c, h * V_AUG + V_HEAD:(h + 1) * V_AUG, :] = ones
    kpe = zkr_ref[...] * ck + zkrot_ref[...] * sinv
    for h in range(MLA_HEADS):
        hs = slice(h * HEAD_PAD, (h + 1) * HEAD_PAD)
        qt_ref[0, hs, :] = (qa_t[hs, :] * cq_t + qb_t[hs, :] * sq_t).astype(BF16)
        k_ref[:, hs] = (kn[:, hs] + kpe).astype(BF16)


def _mla_prep(z, zkr, pos, qg, kvg, wqa_t, wqb_t, wk, wv_t, freq):
    s = z.shape[0]
    tm = min(ATTN_Q_BLOCK, s)
    tk = ATTN_K_BLOCK
    assert s % tm == 0 and tm % (2 * tk) == 0
    nb = s // tm
    hw = MLA_HEADS * HEAD_PAD
    vw = MLA_HEADS * V_AUG
    c0 = 2 * LRU_WIDTH
    full = lambda a: pl.BlockSpec(a.shape, lambda i: (0, 0))
    return pl.pallas_call(
        _mla_prep_body,
        grid=(nb,),
        in_specs=[
            pl.BlockSpec((tm, Q_LORA), lambda i: (i, c0 // Q_LORA)),
            pl.BlockSpec((tm, KV_LORA), lambda i: (i, (c0 + Q_LORA) // KV_LORA)),
            pl.BlockSpec((tm, HEAD_PAD), lambda i: (i, 0)),
            pl.BlockSpec((tm, HEAD_PAD), lambda i: (i, 1)),
            pl.BlockSpec((1, tm), lambda i: (0, i)),
            full(qg), full(kvg), full(wqa_t), full(wqb_t), full(wk), full(wv_t), full(freq),
        ],
        out_specs=[
            pl.BlockSpec((1, hw, tm), lambda i: (i, 0, 0)),
            pl.BlockSpec((tm, hw), lambda i: (i, 0)),
            pl.BlockSpec((tm // tk, vw, tk), lambda i: (i, 0, 0)),
        ],
        out_shape=[
            jax.ShapeDtypeStruct((nb, hw, tm), BF16),
            jax.ShapeDtypeStruct((s, hw), BF16),
            jax.ShapeDtypeStruct((s // tk, vw, tk), BF16),
        ],
        compiler_params=_params("parallel"),
        name="mla_prep",
    )(z, z, zkr, zkr, pos, qg, kvg, wqa_t, wqb_t, wk, wv_t, freq)


_AttnHead = collections.namedtuple("_AttnHead", "s_buf smax_buf p_buf acc qt kcols vrows orows")


def _attn_body(qt_ref, k_ref, vt_ref, o_ref, *scratch):
    ns = ATTN_Q_BLOCK // ATTN_K_BLOCK
    ahead = ns - 2
    per_head = 2 * ns + 3
    tq = qt_ref.shape[2]
    tk = scratch[0].shape[0]
    i = pl.program_id(1)
    neg = jnp.finfo(F32).min

    def head(hh):
        sc = scratch[hh * per_head:(hh + 1) * per_head]
        return _AttnHead(
            s_buf=sc[:ns], smax_buf=sc[ns:2 * ns], p_buf=sc[2 * ns:2 * ns + 2], acc=sc[-1],
            qt=qt_ref[0, hh * HEAD_PAD:(hh + 1) * HEAD_PAD, :],
            kcols=slice(hh * HEAD_PAD, (hh + 1) * HEAD_PAD), vrows=slice(hh * V_AUG, (hh + 1) * V_AUG),
            orows=slice(hh * V_HEAD, (hh + 1) * V_HEAD))

    heads = [head(hh) for hh in range(ATTN_HEADS)]

    def qk(hd, b, c, lane0=0):
        off = pl.multiple_of(b * tk, tk)
        s = jnp.dot(k_ref[pl.ds(off, tk), hd.kcols], hd.qt[:, lane0:], preferred_element_type=F32)
        hd.s_buf[c % ns][:, lane0:] = s
        hd.smax_buf[c % ns][:, lane0:] = _fold_rows(s, jnp.maximum)

    def softmax(hd, c, m, key_off=None, lane0=0):
        sb = hd.s_buf[c % ns]
        p_buf = hd.p_buf[c % 2]
        chunks = range(0, tk, ATTN_ROWS)

        def load(r):
            s = sb[r:r + ATTN_ROWS, lane0:]
            if key_off is None:
                return s
            key = lax.broadcasted_iota(jnp.int32, s.shape, 0) + (key_off + r)
            qry = lax.broadcasted_iota(jnp.int32, s.shape, 1) + lane0
            return jnp.where(key <= qry, s, neg)

        if key_off is None:
            smax = hd.smax_buf[c % ns][...]
        else:
            smax = functools.reduce(jnp.maximum, [_fold_rows(load(r), jnp.maximum) for r in chunks])
        m_old = m[:, lane0:]
        m_new = jnp.maximum(m_old, jnp.max(smax, axis=0, keepdims=True))
        for r in chunks:
            p_buf[r:r + ATTN_ROWS, lane0:] = jnp.exp2(load(r) - m_new).astype(BF16)
        alpha = jnp.exp2(m_old - m_new)
        if lane0:
            p_buf[:, :lane0] = jnp.zeros((tk, lane0), BF16)
            m_new = jnp.concatenate([m[:, :lane0], m_new], axis=1)
            alpha = jnp.concatenate([jnp.ones((1, lane0), F32), alpha], axis=1)
        return m_new, alpha

    def pv(hd, b, c, alpha):
        hd.acc[...] = alpha * hd.acc[...] + jnp.dot(vt_ref[b, hd.vrows, :], hd.p_buf[c % 2][...],
                                                    preferred_element_type=F32)

    def trip(t, carry):
        carry = list(carry)
        for c in range(ns):
            b = t * ns + c
            for hh, hd in enumerate(heads):
                m, alpha = carry[hh]
                qk(hd, b + ahead, c + ahead)
                pv(hd, jnp.maximum(b - 1, 0), c - 1, alpha)
                carry[hh] = softmax(hd, c, m)
        return tuple(carry)

    for hd in heads:
        for c in range(ahead):
            qk(hd, c, c)
        hd.p_buf[1][...] = jnp.zeros_like(hd.p_buf[1])
        hd.acc[...] = jnp.zeros_like(hd.acc)
    init = (jnp.full((1, tq), -jnp.inf, F32), jnp.ones((1, tq), F32))
    carry = list(lax.fori_loop(0, i, trip, (init,) * ATTN_HEADS))

    b0 = i * ns
    for c in range(ns):
        for hh, hd in enumerate(heads):
            m, alpha = carry[hh]
            if c + ahead < ns:
                qk(hd, b0 + c + ahead, c + ahead, (c + ahead) * tk)
            pv(hd, jnp.maximum(b0 + c - 1, 0), c - 1, alpha)
            carry[hh] = softmax(hd, c, m, c * tk, c * tk)
    for hh, hd in enumerate(heads):
        pv(hd, b0 + ns - 1, ns - 1, carry[hh][1])
        o_ref[hd.orows, :] = (hd.acc[:V_HEAD, :] / hd.acc[V_HEAD:V_HEAD + 1, :]).astype(o_ref.dtype)


def _attention(qt, k, vt):
    nb, hw, tq = qt.shape
    nkb, vw, tk = vt.shape
    s = k.shape[0]
    hp = ATTN_HEADS
    per_head = ([pltpu.VMEM((tk, tq), F32)] * (tq // tk) + [pltpu.VMEM((SUBLANES, tq), F32)] * (tq // tk)
                + [pltpu.VMEM((tk, tq), BF16)] * 2 + [pltpu.VMEM((V_AUG, tq), F32)])
    return pl.pallas_call(
        _attn_body,
        grid=(MLA_HEADS // hp, nb),
        in_specs=[
            pl.BlockSpec((1, hp * HEAD_PAD, tq), lambda g, i: (i, g, 0)),
            pl.BlockSpec((s, hp * HEAD_PAD), lambda g, i: (0, g)),
            pl.BlockSpec((nkb, hp * V_AUG, tk), lambda g, i: (0, g, 0)),
        ],
        out_specs=pl.BlockSpec((hp * V_HEAD, tq), lambda g, i: (g, i)),
        out_shape=jax.ShapeDtypeStruct((MLA_HEADS * V_HEAD, s), BF16),
        scratch_shapes=per_head * hp,
        compiler_params=_params("parallel", "parallel"),
        name="attention",
    )(qt, k, vt)


def _ffn_body(*refs, transposed, final_norm):
    n_mix = len(transposed)
    res_ref = refs[0]
    y_refs, w_refs = refs[1:1 + n_mix], refs[1 + n_mix:1 + 2 * n_mix]
    g_ref, wup_ref, cw_ref, cb_ref, wdn_ref, fg_ref, o_ref, tail_s, act_s = refs[1 + 2 * n_mix:]

    @pl.when(pl.program_id(0) == 0)
    def _():
        tail_s[...] = jnp.zeros_like(tail_s)

    tm = res_ref.shape[0]
    x = res_ref[...]
    for y_ref, w_ref, tr in zip(y_refs, w_refs, transposed):
        dims = (((0,), (0,)), ((), ())) if tr else (((1,), (0,)), ((), ()))
        x = x + lax.dot_general(y_ref[...], w_ref[...], dims, preferred_element_type=F32)
    xn = _rms(x, g_ref[...]).astype(BF16)

    def conv(col):
        cs = pl.ds(col, FFN_CHUNK)
        u = jnp.dot(xn, wup_ref[:, cs], preferred_element_type=F32)
        tail = tail_s[:, cs]
        tail_s[:, cs] = u[tm - SUBLANES:, :]
        y = cb_ref[:, cs] + _shift_rows(u, tail, 2) * cw_ref[0:1, cs]
        y = y + _shift_rows(u, tail, 1) * cw_ref[1:2, cs]
        return y + u * cw_ref[2:3, cs]

    for c in range(D_FF // FFN_CHUNK):
        gate = conv(c * FFN_CHUNK)
        val = conv(D_FF + c * FFN_CHUNK)
        act_s[:, pl.ds(c * FFN_CHUNK, FFN_CHUNK)] = (jax.nn.silu(gate) * val).astype(BF16)

    out = x + jnp.dot(act_s[...], wdn_ref[...], preferred_element_type=F32)
    if final_norm:
        out = _rms(out, fg_ref[...])
    o_ref[...] = out


def _ffn(res, ys, ws, transposed, layer, g, wup, cw, cb, wdn, fg, final_norm):
    s, d = res.shape
    tm = min(ROW_BLOCK, s)
    const = lambda a: pl.BlockSpec(a.shape, lambda i: (0, 0), pipeline_mode=pl.Buffered(1))
    stacked = lambda a: pl.BlockSpec((None,) + a.shape[1:], lambda i: (layer, 0, 0), pipeline_mode=pl.Buffered(1))
    y_specs = [pl.BlockSpec((y.shape[0], tm), lambda i: (0, i)) if tr else
               pl.BlockSpec((tm, y.shape[1]), lambda i: (i, 0)) for y, tr in zip(ys, transposed)]
    return pl.pallas_call(
        functools.partial(_ffn_body, transposed=transposed, final_norm=final_norm),
        grid=(s // tm,),
        in_specs=[pl.BlockSpec((tm, d), lambda i: (i, 0))] + y_specs + [const(w) for w in ws]
        + [stacked(g), stacked(wup), stacked(cw), stacked(cb), stacked(wdn), const(fg)],
        out_specs=pl.BlockSpec((tm, d), lambda i: (i, 0)),
        out_shape=jax.ShapeDtypeStruct((s, d), F32),
        scratch_shapes=[pltpu.VMEM((SUBLANES, 2 * D_FF), F32), pltpu.VMEM((tm, D_FF), BF16)],
        compiler_params=_params("arbitrary"),
        name="ffn_final" if final_norm else "ffn",
    )(res, *ys, *ws, g, wup, cw, cb, wdn, fg)


def _mlstm_body(q_ref, k_ref, v_ref, o_ref, gate_ref, gb_ref, ng_ref, y_ref, *state):
    c_s, m_s = state[:M_HEADS], state[M_HEADS:]

    @pl.when(pl.program_id(0) == 0)
    def _():
        for ref in state:
            ref[...] = jnp.zeros_like(ref)

    L = q_ref.shape[0]
    capped = GATE_CAP * jnp.tanh((gate_ref[...] + gb_ref[...]) / GATE_CAP)
    log_f = jax.nn.log_sigmoid(capped)
    row = lax.broadcasted_iota(jnp.int32, (L, LANES), 0)
    b_all = log_f
    d = 1
    while d < L:
        b_all = b_all + jnp.where(row >= d, pltpu.roll(b_all, d, axis=0), 0.0)
        d *= 2
    i_t = capped.T
    b_t = b_all.T

    tri = lax.broadcasted_iota(jnp.int32, (L, L), 1) <= lax.broadcasted_iota(jnp.int32, (L, L), 0)
    ones = jnp.ones((L, LANES), BF16)
    q_scale = M_QK ** -0.5
    heads = range(M_HEADS)
    qs = [q_ref[:, h * M_QK:(h + 1) * M_QK] for h in heads]
    ks = [k_ref[:, h * M_QK:(h + 1) * M_QK] for h in heads]
    vaugs = [jnp.concatenate([v_ref[:, h * M_V:(h + 1) * M_V], ones], axis=1) for h in heads]
    c_sts = [c_s[h][...] for h in heads]
    m_sts = [m_s[h][0:1, :] for h in heads]
    qks = [lax.dot_general(qs[h], ks[h], (((1,), (1,)), ((), ())), preferred_element_type=F32) for h in heads]
    qcs = [jnp.dot(qs[h], c_sts[h].astype(BF16), preferred_element_type=F32) for h in heads]

    lanes = lambda col: jnp.broadcast_to(col, (L, LANES))
    wide = lambda a, n: jnp.concatenate([a] * n, axis=1)

    gate = []
    for h in heads:
        b_c = lanes(b_all[:, M_HEADS + h:M_HEADS + h + 1])
        i_c = lanes(capped[:, h:h + 1])
        b_row = b_t[M_HEADS + h:M_HEADS + h + 1, :]
        i_row = i_t[h:h + 1, :]
        dmat = jnp.where(tri, wide(b_c, L // LANES) - b_row + i_row, -jnp.inf)
        inter = b_c + m_sts[h]
        m_t = jnp.maximum(inter, lanes(jnp.max(dmat, axis=-1, keepdims=True)))
        w = jnp.exp(dmat - wide(m_t, L // LANES)) * q_scale
        g = jnp.exp(inter - m_t) * q_scale
        b_last = b_c[L - 1:, :]
        w_end = b_last - b_c + i_c
        m_new = jnp.maximum(b_last + m_sts[h], jnp.max(w_end, axis=0, keepdims=True))
        g_end = jnp.exp(b_last + m_sts[h] - m_new)
        gate.append((w, g, m_t, m_new, g_end, jnp.exp(w_end - m_new)))

    n_aug = (M_V + LANES) // LANES
    nums = []
    for h in heads:
        w, g, _, m_new, g_end, w_s = gate[h]
        sc = (qks[h] * w).astype(BF16)
        nums.append(wide(g, n_aug) * qcs[h] + jnp.dot(sc, vaugs[h], preferred_element_type=F32))
        kw = (ks[h].astype(F32) * w_s).astype(BF16)
        c_s[h][...] = wide(g_end, n_aug) * c_sts[h] + lax.dot_general(
            kw, vaugs[h], (((0,), (0,)), ((), ())), preferred_element_type=F32)
        m_s[h][...] = jnp.broadcast_to(m_new, m_s[h].shape)

    for h in heads:
        m_t = gate[h][2]
        den = nums[h][:, M_V:]
        hh = nums[h][:, :M_V] / wide(jnp.maximum(jnp.abs(den), jnp.exp(-m_t)), M_V // LANES)
        msq = lanes(jnp.mean(hh * hh, axis=-1, keepdims=True))
        hn = hh * wide(lax.rsqrt(msq + EPS), M_V // LANES)
        vs = slice(h * M_V, (h + 1) * M_V)
        y_ref[:, vs] = (hn * ng_ref[:, vs] * jax.nn.sigmoid(o_ref[:, vs])).astype(y_ref.dtype)


def _mlstm(qkv, og, gate_bias, norm_g):
    s = qkv.shape[0]
    L = min(M_CHUNK, s)
    qk = M_HEADS * M_QK
    vw = M_HEADS * M_V
    return pl.pallas_call(
        _mlstm_body,
        grid=(s // L,),
        in_specs=[
            pl.BlockSpec((L, qk), lambda i: (i, 0)),
            pl.BlockSpec((L, qk), lambda i: (i, 1)),
            pl.BlockSpec((L, vw), lambda i: (i, 2 * qk // vw)),
            pl.BlockSpec((L, vw), lambda i: (i, 0)),
            pl.BlockSpec((L, LANES), lambda i: (i, vw // LANES)),
            pl.BlockSpec((1, LANES), lambda i: (0, 0)),
            pl.BlockSpec((1, vw), lambda i: (0, 0)),
        ],
        out_specs=pl.BlockSpec((L, vw), lambda i: (i, 0)),
        out_shape=jax.ShapeDtypeStruct((s, vw), BF16),
        scratch_shapes=[pltpu.VMEM((M_QK, M_V + LANES), F32)] * M_HEADS + [pltpu.VMEM((SUBLANES, LANES), F32)] * M_HEADS,
        compiler_params=_params("arbitrary"),
        name="mlstm",
    )(qkv, qkv, qkv, og, og, gate_bias, norm_g)


def _rot_cols(w):
    half = QK_ROPE // 2
    return jnp.concatenate([-w[:, half:], w[:, :half]], axis=1)


def _place(w, start):
    return jnp.pad(w, ((0, 0), (start, HEAD_PAD - start - w.shape[1])))


def _block_diag(w):
    g, bi, bo = w.shape
    eye = jnp.eye(g, dtype=w.dtype)
    return (w[:, :, None, :] * eye[:, None, :, None]).reshape(g * bi, g * bo)


def kernel(x, positions, e_norm_g, e_w_in, e_lru_conv_w, e_lru_conv_b, e_lru_w_a, e_lru_b_a, e_lru_w_x,
           e_lru_b_x, e_lru_lambda, e_q_norm_g, e_w_qb, e_kv_norm_g, e_w_kvb, e_w_out, o_norm_g, o_w_in,
           o_b_igate, o_b_fgate, o_out_norm_g, o_w_out, f_norm_g, f_w_up, f_conv_w, f_conv_b, f_w_down,
           final_norm_g):
    bsz, s, d = x.shape
    assert bsz == 1 and d == D_MODEL
    h = x.reshape(s, d)
    pos = positions.reshape(1, s)
    depth = f_norm_g.shape[0]

    half = QK_ROPE // 2
    inv_freq = ROPE_THETA ** (-jnp.arange(half, dtype=F32) / half)
    freq = inv_freq.reshape(half, 1)
    f_w_up_b = f_w_up.astype(BF16)
    f_w_down_b = f_w_down.astype(BF16)

    for layer in range(depth):
        j = layer // 2
        if layer % 2 == 0:
            w_in = e_w_in[j]
            c_kr = 2 * LRU_WIDTH + Q_LORA + KV_LORA
            w_kr = w_in[:, c_kr:c_kr + QK_ROPE]
            w_kr_placed = jnp.concatenate([_place(w_kr, QK_NOPE), _place(_rot_cols(w_kr), QK_NOPE)], axis=1)
            z, zkr = _norm_matmul(h, e_norm_g[j], [w_in[:, :c_kr].astype(BF16), w_kr_placed.astype(BF16)],
                                  (F32, F32), "even_in")

            y_lru = _rglru(
                z, e_lru_conv_w[j], e_lru_conv_b[j][None, :],
                _block_diag(e_lru_w_a[j]).astype(BF16), e_lru_b_a[j].reshape(1, LRU_WIDTH),
                _block_diag(e_lru_w_x[j]).astype(BF16), e_lru_b_x[j].reshape(1, LRU_WIDTH),
                e_lru_lambda[j][None, :])

            wq = e_w_qb[j].reshape(Q_LORA, MLA_HEADS, QK_NOPE + QK_ROPE)
            wq_a = jnp.pad(wq, ((0, 0), (0, 0), (0, HEAD_PAD - QK_NOPE - QK_ROPE)))
            wq_pe = wq[:, :, QK_NOPE:]
            wq_rot = jnp.concatenate([-wq_pe[:, :, half:], wq_pe[:, :, :half]], axis=2)
            wq_b = jnp.pad(wq_rot, ((0, 0), (0, 0), (QK_NOPE, HEAD_PAD - QK_NOPE - QK_ROPE)))
            wkv = e_w_kvb[j].reshape(KV_LORA, MLA_HEADS, QK_NOPE + V_HEAD)
            wk = jnp.pad(wkv[:, :, :QK_NOPE], ((0, 0), (0, 0), (0, HEAD_PAD - QK_NOPE)))
            hw = MLA_HEADS * HEAD_PAD
            qt, k, vt = _mla_prep(
                z, zkr, pos, e_q_norm_g[j][None, :], e_kv_norm_g[j][None, :],
                wq_a.reshape(Q_LORA, hw).T.astype(BF16), wq_b.reshape(Q_LORA, hw).T.astype(BF16),
                wk.reshape(KV_LORA, hw).astype(BF16),
                wkv[:, :, QK_NOPE:].reshape(KV_LORA, MLA_HEADS * V_HEAD).T.astype(BF16), freq)
            y_mla_t = _attention(qt, k, vt)

            w_out = e_w_out[j].astype(BF16)
            mix = ([y_lru, y_mla_t], [w_out[:LRU_WIDTH], w_out[LRU_WIDTH:]], (False, True))
        else:
            w_in = o_w_in[j]
            w_og = jnp.pad(w_in[:, ODD_QKV_COLS:], ((0, 0), (0, ODD_COLS - w_in.shape[1])))
            qkv, og = _norm_matmul(h, o_norm_g[j], [w_in[:, :ODD_QKV_COLS].astype(BF16), w_og.astype(BF16)],
                                   (BF16, F32), "odd_in")
            gate_bias = jnp.pad(jnp.concatenate([o_b_igate[j], o_b_fgate[j]])[None, :],
                                ((0, 0), (0, LANES - 2 * M_HEADS)))
            y = _mlstm(qkv, og, gate_bias, o_out_norm_g[j][None, :])
            mix = ([y], [o_w_out[j].astype(BF16)], (False,))

        h = _ffn(h, *mix, layer, f_norm_g[:, None, :], f_w_up_b, f_conv_w, f_conv_b[:, None, :], f_w_down_b,
                 final_norm_g[None, :], final_norm=(layer == depth - 1))
    return h.reshape(bsz, s, d)
```

```python
import collections
import functools

import jax
import jax.numpy as jnp
from jax import lax
from jax.experimental import pallas as pl
from jax.experimental.pallas import tpu as pltpu

F32 = jnp.float32
BF16 = jnp.bfloat16

EPS = 1e-6
LANES = 128
SUBLANES = 8
VMEM_LIMIT = 56 * 1024 * 1024

D_MODEL = 1024
LRU_WIDTH = 512
LRU_BLOCKS = 8
LRU_CONV = 4
LRU_C = 8.0
MLA_HEADS = 8
Q_LORA = 256
KV_LORA = 128
QK_NOPE = 64
QK_ROPE = 32
V_HEAD = 64
V_AUG = 80
ROPE_THETA = 10000.0
HEAD_PAD = 128
M_HEADS = 4
M_QK = 128
M_V = 256
GATE_CAP = 15.0
D_FF = 2816
FFN_CONV = 3

EVEN_COLS = 2 * LRU_WIDTH + Q_LORA + KV_LORA + 2 * HEAD_PAD
ODD_QKV_COLS = 2 * M_HEADS * M_QK + M_HEADS * M_V
ODD_COLS = ODD_QKV_COLS + M_HEADS * M_V + LANES

ROW_BLOCK = 512
ATTN_Q_BLOCK = 1024
ATTN_K_BLOCK = 256
ATTN_ROWS = 32
ATTN_HEADS = 2
ATTN_PV_LAG = 1
LOG2E = 1.4426950408889634
M_CHUNK = 256
FFN_CHUNK = 256
FFN_ROW_SPLIT = 2


def _params(*sem):
    return pltpu.CompilerParams(dimension_semantics=sem, vmem_limit_bytes=VMEM_LIMIT)


def _rms(x, g):
    inv = lax.rsqrt(jnp.mean(x * x, axis=-1, keepdims=True) + EPS)
    return x * inv * g


def _fold_rows(x, op):
    r = x.shape[0]
    while r > SUBLANES:
        r //= 2
        x = op(x[:r], x[r:])
    return x


def _shift_rows(x, prev_tail, sh):
    t, w = x.shape
    x3 = jnp.concatenate([prev_tail, x], axis=0).reshape(t // SUBLANES + 1, SUBLANES, w)
    rolled = pltpu.roll(x3, sh, axis=1)
    sub = lax.broadcasted_iota(jnp.int32, (t // SUBLANES, SUBLANES, w), 1)
    return jnp.where(sub >= sh, rolled[1:], rolled[:-1]).reshape(t, w)


def _norm_matmul_body(x_ref, g_ref, *refs):
    w_refs, o_refs = refs[:len(refs) // 2], refs[len(refs) // 2:]
    xn = _rms(x_ref[...], g_ref[...]).astype(BF16)
    for w_ref, o_ref in zip(w_refs, o_refs):
        o_ref[...] = jnp.dot(xn, w_ref[...], preferred_element_type=F32).astype(o_ref.dtype)


def _norm_matmul(x, g, ws, out_dtypes, name):
    s, d = x.shape
    tm = min(ROW_BLOCK, s)
    return pl.pallas_call(
        _norm_matmul_body,
        grid=(s // tm,),
        in_specs=[pl.BlockSpec((tm, d), lambda i: (i, 0)), pl.BlockSpec((1, d), lambda i: (0, 0))]
        + [pl.BlockSpec(w.shape, lambda i: (0, 0)) for w in ws],
        out_specs=[pl.BlockSpec((tm, w.shape[1]), lambda i: (i, 0)) for w in ws],
        out_shape=[jax.ShapeDtypeStruct((s, w.shape[1]), dt) for w, dt in zip(ws, out_dtypes)],
        compiler_params=_params("parallel"),
        name=name,
    )(x, g.reshape(1, d), *ws)


def _rglru_body(xr_ref, gate_ref, cw_ref, cb_ref, wa_ref, ba_ref, wx_ref, bx_ref, lam_ref,
                o_ref, tail_s, h_s, a_s, b_s):
    @pl.when(pl.program_id(0) == 0)
    def _():
        tail_s[...] = jnp.zeros_like(tail_s)
        h_s[...] = jnp.zeros_like(h_s)

    tm = xr_ref.shape[0]
    xr = xr_ref[...]
    tail = tail_s[...]
    tail_s[...] = xr[tm - SUBLANES:, :]
    xc = cb_ref[...] + _shift_rows(xr, tail, 3) * cw_ref[0:1, :]
    xc = xc + _shift_rows(xr, tail, 2) * cw_ref[1:2, :]
    xc = xc + _shift_rows(xr, tail, 1) * cw_ref[2:3, :]
    xc = xc + xr * cw_ref[3:4, :]

    xcb = xc.astype(BF16)
    r = jax.nn.sigmoid(jnp.dot(xcb, wa_ref[...], preferred_element_type=F32) + ba_ref[...])
    i = jax.nn.sigmoid(jnp.dot(xcb, wx_ref[...], preferred_element_type=F32) + bx_ref[...])
    neg_lam = -lam_ref[...]
    softplus = jnp.maximum(neg_lam, 0.0) + jnp.log1p(jnp.exp(-jnp.abs(neg_lam)))
    log_a = (-LRU_C * r) * softplus
    a = jnp.exp(log_a)
    th = jnp.tanh(log_a)
    y = -2.0 * th
    root = jnp.where(y > 0.0, y * lax.rsqrt(y), 0.0)
    u = root * lax.rsqrt(1.0 - th) * (i * xc)

    w = a.shape[1]
    a = a.reshape(tm // SUBLANES, SUBLANES, w)
    u = u.reshape(tm // SUBLANES, SUBLANES, w)
    sub = lax.broadcasted_iota(jnp.int32, a.shape, 1)
    for d in (1, 2, 4):
        keep = sub >= d
        a_sh = jnp.where(keep, pltpu.roll(a, d, axis=1), 1.0)
        u_sh = jnp.where(keep, pltpu.roll(u, d, axis=1), 0.0)
        u = a * u_sh + u
        a = a * a_sh
    a_s[...] = a.reshape(tm, w)
    b_s[...] = u.reshape(tm, w)

    def group(k, h):
        off = pl.multiple_of(k * SUBLANES, SUBLANES)
        ht = a_s[pl.ds(off, SUBLANES), :] * h + b_s[pl.ds(off, SUBLANES), :]
        b_s[pl.ds(off, SUBLANES), :] = ht
        return ht[SUBLANES - 1:, :]

    h_s[...] = lax.fori_loop(0, tm // SUBLANES, group, h_s[...], unroll=8)
    o_ref[...] = (b_s[...] * jax.nn.gelu(gate_ref[...])).astype(o_ref.dtype)


def _rglru(z, cw, cb, wa, ba, wx, bx, lam):
    s = z.shape[0]
    w = LRU_WIDTH
    tm = min(ROW_BLOCK, s)
    vec = lambda r: pl.BlockSpec((r, w), lambda i: (0, 0))
    return pl.pallas_call(
        _rglru_body,
        grid=(s // tm,),
        in_specs=[
            pl.BlockSpec((tm, w), lambda i: (i, 0)),
            pl.BlockSpec((tm, w), lambda i: (i, 1)),
            vec(LRU_CONV), vec(1), vec(w), vec(1), vec(w), vec(1), vec(1),
        ],
        out_specs=pl.BlockSpec((tm, w), lambda i: (i, 0)),
        out_shape=jax.ShapeDtypeStruct((s, w), BF16),
        scratch_shapes=[
            pltpu.VMEM((SUBLANES, w), F32),
            pltpu.VMEM((1, w), F32),
            pltpu.VMEM((tm, w), F32),
            pltpu.VMEM((tm, w), F32),
        ],
        compiler_params=_params("arbitrary"),
        name="rglru",
    )(z, z, cw, cb, wa, ba, wx, bx, lam)


def _mla_prep_body(zq_ref, zkv_ref, zkr_ref, zkrot_ref, pos_ref, qg_ref, kvg_ref,
                   wqa_ref, wqb_ref, wk_ref, wv_ref, freq_ref, qt_ref, k_ref, vt_ref):
    nt = (((1,), (1,)), ((), ()))
    tm = pos_ref.shape[1]
    ang = freq_ref[...] * pos_ref[...].astype(F32)
    cos_h, sin_h = jnp.cos(ang), jnp.sin(ang)
    zeros = lambda r: jnp.zeros((r, tm), F32)
    cos_t = jnp.concatenate([jnp.ones((QK_NOPE, tm), F32), cos_h, cos_h, zeros(HEAD_PAD - QK_NOPE - QK_ROPE)], axis=0)
    sin_t = jnp.concatenate([zeros(QK_NOPE), sin_h, sin_h, zeros(HEAD_PAD - QK_NOPE - QK_ROPE)], axis=0)
    scale = (QK_NOPE + QK_ROPE) ** -0.5 * LOG2E
    cq_t = cos_t * scale
    sq_t = sin_t * scale
    ck = cos_t.T
    sinv = sin_t.T

    qn = _rms(zq_ref[...], qg_ref[...]).astype(BF16)
    qa_t = lax.dot_general(wqa_ref[...], qn, nt, preferred_element_type=F32)
    qb_t = lax.dot_general(wqb_ref[...], qn, nt, preferred_element_type=F32)
    kvn = _rms(zkv_ref[...], kvg_ref[...]).astype(BF16)
    kn = jnp.dot(kvn, wk_ref[...], preferred_element_type=F32)
    v_t = lax.dot_general(wv_ref[...], kvn, nt, preferred_element_type=F32).astype(BF16)
    tk = vt_ref.shape[2]
    ones = jnp.ones((V_AUG - V_HEAD, tk), BF16)
    for c in range(vt_ref.shape[0]):
        for h in range(MLA_HEADS):
            vt_ref[c, h * V_AUG:h * V_AUG + V_HEAD, :] = v_t[h * V_HEAD:(h + 1) * V_HEAD, c * tk:(c + 1) * tk]
            vt_ref[c, h * V_AUG + V_HEAD:(h + 1) * V_AUG, :] = ones
    kpe = zkr_ref[...] * ck + zkrot_ref[...] * sinv
    for h in range(MLA_HEADS):
        hs = slice(h * HEAD_PAD, (h + 1) * HEAD_PAD)
        qt_ref[0, hs, :] = (qa_t[hs, :] * cq_t + qb_t[hs, :] * sq_t).astype(BF16)
        k_ref[:, hs] = (kn[:, hs] + kpe).astype(BF16)


def _mla_prep(z, zkr, pos, qg, kvg, wqa_t, wqb_t, wk, wv_t, freq):
    s = z.shape[0]
    tm = min(ATTN_Q_BLOCK, s)
    tk = ATTN_K_BLOCK
    assert s % tm == 0 and tm % (2 * tk) == 0
    nb = s // tm
    hw = MLA_HEADS * HEAD_PAD
    vw = MLA_HEADS * V_AUG
    c0 = 2 * LRU_WIDTH
    full = lambda a: pl.BlockSpec(a.shape, lambda i: (0, 0))
    return pl.pallas_call(
        _mla_prep_body,
        grid=(nb,),
        in_specs=[
            pl.BlockSpec((tm, Q_LORA), lambda i: (i, c0 // Q_LORA)),
            pl.BlockSpec((tm, KV_LORA), lambda i: (i, (c0 + Q_LORA) // KV_LORA)),
            pl.BlockSpec((tm, HEAD_PAD), lambda i: (i, 0)),
            pl.BlockSpec((tm, HEAD_PAD), lambda i: (i, 1)),
            pl.BlockSpec((1, tm), lambda i: (0, i)),
            full(qg), full(kvg), full(wqa_t), full(wqb_t), full(wk), full(wv_t), full(freq),
        ],
        out_specs=[
            pl.BlockSpec((1, hw, tm), lambda i: (i, 0, 0)),
            pl.BlockSpec((tm, hw), lambda i: (i, 0)),
            pl.BlockSpec((tm // tk, vw, tk), lambda i: (i, 0, 0)),
        ],
        out_shape=[
            jax.ShapeDtypeStruct((nb, hw, tm), BF16),
            jax.ShapeDtypeStruct((s, hw), BF16),
            jax.ShapeDtypeStruct((s // tk, vw, tk), BF16),
        ],
        compiler_params=_params("parallel"),
        name="mla_prep",
    )(z, z, zkr, zkr, pos, qg, kvg, wqa_t, wqb_t, wk, wv_t, freq)


_AttnHead = collections.namedtuple("_AttnHead", "s_buf smax_buf p_buf acc qt kcols vrows orows")


def _attn_body(qt_ref, k_ref, vt_ref, o_ref, *scratch):
    ns = ATTN_Q_BLOCK // ATTN_K_BLOCK
    ahead = ns - 2
    lag = ATTN_PV_LAG
    npb = lag + 1
    per_head = 2 * ns + npb + 1
    tq = qt_ref.shape[2]
    tk = scratch[0].shape[0]
    i = pl.program_id(1)
    neg = jnp.finfo(F32).min

    def head(hh):
        sc = scratch[hh * per_head:(hh + 1) * per_head]
        return _AttnHead(
            s_buf=sc[:ns], smax_buf=sc[ns:2 * ns], p_buf=sc[2 * ns:2 * ns + npb], acc=sc[-1],
            qt=qt_ref[0, hh * HEAD_PAD:(hh + 1) * HEAD_PAD, :],
            kcols=slice(hh * HEAD_PAD, (hh + 1) * HEAD_PAD), vrows=slice(hh * V_AUG, (hh + 1) * V_AUG),
            orows=slice(hh * V_HEAD, (hh + 1) * V_HEAD))

    heads = [head(hh) for hh in range(ATTN_HEADS)]

    def qk(hd, b, c, lane0=0):
        off = pl.multiple_of(b * tk, tk)
        s = jnp.dot(k_ref[pl.ds(off, tk), hd.kcols], hd.qt[:, lane0:], preferred_element_type=F32)
        hd.s_buf[c % ns][:, lane0:] = s
        hd.smax_buf[c % ns][:, lane0:] = _fold_rows(s, jnp.maximum)

    def softmax(hd, c, m, lane0=None):
        diag = lane0 is not None
        lane0 = lane0 or 0
        sb = hd.s_buf[c % ns]
        p_buf = hd.p_buf[c % npb]
        chunks = range(0, tk, ATTN_ROWS)
        rest = slice(lane0 + tk, tq)
        join = lambda parts: jnp.concatenate(parts, axis=1) if len(parts) > 1 else parts[0]

        def tri(r):
            s = sb[r:r + ATTN_ROWS, lane0:lane0 + tk]
            key = lax.broadcasted_iota(jnp.int32, s.shape, 0) + r
            return jnp.where(key <= lax.broadcasted_iota(jnp.int32, s.shape, 1), s, neg)

        def load(r):
            if not diag:
                return sb[r:r + ATTN_ROWS, :]
            return join([tri(r)] + ([sb[r:r + ATTN_ROWS, rest]] if rest.start < tq else []))

        if not diag:
            smax = hd.smax_buf[c % ns][...]
        else:
            smax_tri = functools.reduce(jnp.maximum, [_fold_rows(tri(r), jnp.maximum) for r in chunks])
            smax = join([smax_tri] + ([hd.smax_buf[c % ns][:, rest]] if rest.start < tq else []))
        m_old = m[:, lane0:]
        m_new = jnp.maximum(m_old, jnp.max(smax, axis=0, keepdims=True))
        for r in chunks:
            p_buf[r:r + ATTN_ROWS, lane0:] = jnp.exp2(load(r) - m_new).astype(BF16)
        alpha = jnp.exp2(m_old - m_new)
        if lane0:
            p_buf[:, :lane0] = jnp.zeros((tk, lane0), BF16)
            m_new = jnp.concatenate([m[:, :lane0], m_new], axis=1)
            alpha = jnp.concatenate([jnp.ones((1, lane0), F32), alpha], axis=1)
        return m_new, alpha

    def pv(hd, b, c, alpha):
        hd.acc[...] = alpha * hd.acc[...] + jnp.dot(vt_ref[b, hd.vrows, :], hd.p_buf[c % npb][...],
                                                    preferred_element_type=F32)

    def step(hd, state, b, c, diag):
        m, alphas = state
        if not diag:
            qk(hd, b + ahead, c + ahead)
        elif c + ahead < ns:
            qk(hd, b + ahead, c + ahead, (c + ahead) * tk)
        pv(hd, jnp.maximum(b - lag, 0), c - lag, alphas[0])
        m, alpha = softmax(hd, c, m, c * tk if diag else None)
        return m, alphas[1:] + (alpha,)

    def trip(t, carry):
        carry = list(carry)
        for c in range(ns):
            for hh, hd in enumerate(heads):
                carry[hh] = step(hd, carry[hh], t * ns + c, c, diag=False)
        return tuple(carry)

    for hd in heads:
        for c in range(ahead):
            qk(hd, c, c)
        for k in range(1, lag + 1):
            hd.p_buf[-k % npb][...] = jnp.zeros_like(hd.p_buf[-k % npb])
        hd.acc[...] = jnp.zeros_like(hd.acc)
    init = (jnp.full((1, tq), -jnp.inf, F32), (jnp.ones((1, tq), F32),) * lag)
    carry = list(lax.fori_loop(0, i, trip, (init,) * ATTN_HEADS))

    b0 = i * ns
    for c in range(ns):
        for hh, hd in enumerate(heads):
            carry[hh] = step(hd, carry[hh], b0 + c, c, diag=True)
    for hh, hd in enumerate(heads):
        for k in range(lag):
            pv(hd, b0 + ns - lag + k, ns - lag + k, carry[hh][1][k])
        o_ref[hd.orows, :] = (hd.acc[:V_HEAD, :] / hd.acc[V_HEAD:V_HEAD + 1, :]).astype(o_ref.dtype)


def _attention(qt, k, vt):
    nb, hw, tq = qt.shape
    nkb, vw, tk = vt.shape
    s = k.shape[0]
    hp = ATTN_HEADS
    per_head = ([pltpu.VMEM((tk, tq), F32)] * (tq // tk) + [pltpu.VMEM((SUBLANES, tq), F32)] * (tq // tk)
                + [pltpu.VMEM((tk, tq), BF16)] * (ATTN_PV_LAG + 1) + [pltpu.VMEM((V_AUG, tq), F32)])
    return pl.pallas_call(
        _attn_body,
        grid=(MLA_HEADS // hp, nb),
        in_specs=[
            pl.BlockSpec((1, hp * HEAD_PAD, tq), lambda g, i: (i, g, 0)),
            pl.BlockSpec((s, hp * HEAD_PAD), lambda g, i: (0, g)),
            pl.BlockSpec((nkb, hp * V_AUG, tk), lambda g, i: (0, g, 0)),
        ],
        out_specs=pl.BlockSpec((hp * V_HEAD, tq), lambda g, i: (g, i)),
        out_shape=jax.ShapeDtypeStruct((MLA_HEADS * V_HEAD, s), BF16),
        scratch_shapes=per_head * hp,
        compiler_params=_params("parallel", "parallel"),
        name="attention",
    )(qt, k, vt)


def _ffn_body(*refs, transposed, final_norm):
    n_mix = len(transposed)
    res_ref = refs[0]
    y_refs, w_refs = refs[1:1 + n_mix], refs[1 + n_mix:1 + 2 * n_mix]
    g_ref, wup_ref, cw_ref, cb_ref, wdn_ref, fg_ref, o_ref, tail_s, act_s, xn_s = refs[1 + 2 * n_mix:]

    @pl.when(pl.program_id(0) == 0)
    def _():
        tail_s[...] = jnp.zeros_like(tail_s)

    tm = res_ref.shape[0]
    x = res_ref[...]
    for y_ref, w_ref, tr in zip(y_refs, w_refs, transposed):
        dims = (((0,), (0,)), ((), ())) if tr else (((1,), (0,)), ((), ()))
        x = x + lax.dot_general(y_ref[...], w_ref[...], dims, preferred_element_type=F32)
    xn_s[...] = _rms(x, g_ref[...]).astype(BF16)

    def conv(rows, col, tail):
        cs = pl.ds(col, FFN_CHUNK)
        u = jnp.dot(xn_s[rows, :], wup_ref[:, cs], preferred_element_type=F32)
        y = cb_ref[:, cs] + _shift_rows(u, tail, 2) * cw_ref[0:1, cs]
        y = y + _shift_rows(u, tail, 1) * cw_ref[1:2, cs]
        return y + u * cw_ref[2:3, cs], u[u.shape[0] - SUBLANES:, :]

    sub = tm // FFN_ROW_SPLIT
    for c in range(D_FF // FFN_CHUNK):
        gcol, vcol = c * FFN_CHUNK, D_FF + c * FFN_CHUNK
        g_tail, v_tail = tail_s[:, pl.ds(gcol, FFN_CHUNK)], tail_s[:, pl.ds(vcol, FFN_CHUNK)]
        for r in range(FFN_ROW_SPLIT):
            rows = pl.ds(r * sub, sub)
            gate, g_tail = conv(rows, gcol, g_tail)
            val, v_tail = conv(rows, vcol, v_tail)
            act_s[rows, pl.ds(c * FFN_CHUNK, FFN_CHUNK)] = (jax.nn.silu(gate) * val).astype(BF16)
        tail_s[:, pl.ds(gcol, FFN_CHUNK)] = g_tail
        tail_s[:, pl.ds(vcol, FFN_CHUNK)] = v_tail

    out = x + jnp.dot(act_s[...], wdn_ref[...], preferred_element_type=F32)
    if final_norm:
        out = _rms(out, fg_ref[...])
    o_ref[...] = out


def _ffn(res, ys, ws, transposed, layer, g, wup, cw, cb, wdn, fg, final_norm):
    s, d = res.shape
    tm = min(ROW_BLOCK, s)
    const = lambda a: pl.BlockSpec(a.shape, lambda i: (0, 0), pipeline_mode=pl.Buffered(1))
    stacked = lambda a: pl.BlockSpec((None,) + a.shape[1:], lambda i: (layer, 0, 0), pipeline_mode=pl.Buffered(1))
    y_specs = [pl.BlockSpec((y.shape[0], tm), lambda i: (0, i)) if tr else
               pl.BlockSpec((tm, y.shape[1]), lambda i: (i, 0)) for y, tr in zip(ys, transposed)]
    return pl.pallas_call(
        functools.partial(_ffn_body, transposed=transposed, final_norm=final_norm),
        grid=(s // tm,),
        in_specs=[pl.BlockSpec((tm, d), lambda i: (i, 0))] + y_specs + [const(w) for w in ws]
        + [stacked(g), stacked(wup), stacked(cw), stacked(cb), stacked(wdn), const(fg)],
        out_specs=pl.BlockSpec((tm, d), lambda i: (i, 0)),
        out_shape=jax.ShapeDtypeStruct((s, d), F32),
        scratch_shapes=[pltpu.VMEM((SUBLANES, 2 * D_FF), F32), pltpu.VMEM((tm, D_FF), BF16),
                        pltpu.VMEM((tm, d), BF16)],
        compiler_params=_params("arbitrary"),
        name="ffn_final" if final_norm else "ffn",
    )(res, *ys, *ws, g, wup, cw, cb, wdn, fg)


def _mlstm_body(q_ref, k_ref, v_ref, o_ref, gate_ref, gb_ref, ng_ref, y_ref, *state):
    c_s, m_s = state[:M_HEADS], state[M_HEADS:]

    @pl.when(pl.program_id(0) == 0)
    def _():
        for ref in state:
            ref[...] = jnp.zeros_like(ref)

    L = q_ref.shape[0]
    capped = GATE_CAP * jnp.tanh((gate_ref[...] + gb_ref[...]) / GATE_CAP)
    log_f = jax.nn.log_sigmoid(capped)
    row = lax.broadcasted_iota(jnp.int32, (L, LANES), 0)
    b_all = log_f
    d = 1
    while d < L:
        b_all = b_all + jnp.where(row >= d, pltpu.roll(b_all, d, axis=0), 0.0)
        d *= 2
    i_t = capped.T
    b_t = b_all.T

    tri = lax.broadcasted_iota(jnp.int32, (L, L), 1) <= lax.broadcasted_iota(jnp.int32, (L, L), 0)
    ones = jnp.ones((L, LANES), BF16)
    q_scale = M_QK ** -0.5
    heads = range(M_HEADS)
    qs = [q_ref[:, h * M_QK:(h + 1) * M_QK] for h in heads]
    ks = [k_ref[:, h * M_QK:(h + 1) * M_QK] for h in heads]
    vaugs = [jnp.concatenate([v_ref[:, h * M_V:(h + 1) * M_V], ones], axis=1) for h in heads]
    c_sts = [c_s[h][...] for h in heads]
    m_sts = [m_s[h][0:1, :] for h in heads]
    qks = [lax.dot_general(qs[h], ks[h], (((1,), (1,)), ((), ())), preferred_element_type=F32) for h in heads]
    qcs = [jnp.dot(qs[h], c_sts[h].astype(BF16), preferred_element_type=F32) for h in heads]

    lanes = lambda col: jnp.broadcast_to(col, (L, LANES))
    wide = lambda a, n: jnp.concatenate([a] * n, axis=1)

    gate = []
    for h in heads:
        b_c = lanes(b_all[:, M_HEADS + h:M_HEADS + h + 1])
        i_c = lanes(capped[:, h:h + 1])
        b_row = b_t[M_HEADS + h:M_HEADS + h + 1, :]
        i_row = i_t[h:h + 1, :]
        dmat = jnp.where(tri, wide(b_c, L // LANES) - b_row + i_row, -jnp.inf)
        inter = b_c + m_sts[h]
        m_t = jnp.maximum(inter, lanes(jnp.max(dmat, axis=-1, keepdims=True)))
        w = jnp.exp(dmat - wide(m_t, L // LANES)) * q_scale
        g = jnp.exp(inter - m_t) * q_scale
        b_last = b_c[L - 1:, :]
        w_end = b_last - b_c + i_c
        m_new = jnp.maximum(b_last + m_sts[h], jnp.max(w_end, axis=0, keepdims=True))
        g_end = jnp.exp(b_last + m_sts[h] - m_new)
        gate.append((w, g, m_t, m_new, g_end, jnp.exp(w_end - m_new)))

    n_aug = (M_V + LANES) // LANES
    nums = []
    for h in heads:
        w, g, _, m_new, g_end, w_s = gate[h]
        sc = (qks[h] * w).astype(BF16)
        nums.append(wide(g, n_aug) * qcs[h] + jnp.dot(sc, vaugs[h], preferred_element_type=F32))
        kw = (ks[h].astype(F32) * w_s).astype(BF16)
        c_s[h][...] = wide(g_end, n_aug) * c_sts[h] + lax.dot_general(
            kw, vaugs[h], (((0,), (0,)), ((), ())), preferred_element_type=F32)
        m_s[h][...] = jnp.broadcast_to(m_new, m_s[h].shape)

    for h in heads:
        m_t = gate[h][2]
        den = nums[h][:, M_V:]
        hh = nums[h][:, :M_V] / wide(jnp.maximum(jnp.abs(den), jnp.exp(-m_t)), M_V // LANES)
        msq = lanes(jnp.mean(hh * hh, axis=-1, keepdims=True))
        hn = hh * wide(lax.rsqrt(msq + EPS), M_V // LANES)
        vs = slice(h * M_V, (h + 1) * M_V)
        y_ref[:, vs] = (hn * ng_ref[:, vs] * jax.nn.sigmoid(o_ref[:, vs])).astype(y_ref.dtype)


def _mlstm(qkv, og, gate_bias, norm_g):
    s = qkv.shape[0]
    L = min(M_CHUNK, s)
    qk = M_HEADS * M_QK
    vw = M_HEADS * M_V
    return pl.pallas_call(
        _mlstm_body,
        grid=(s // L,),
        in_specs=[
            pl.BlockSpec((L, qk), lambda i: (i, 0)),
            pl.BlockSpec((L, qk), lambda i: (i, 1)),
            pl.BlockSpec((L, vw), lambda i: (i, 2 * qk // vw)),
            pl.BlockSpec((L, vw), lambda i: (i, 0)),
            pl.BlockSpec((L, LANES), lambda i: (i, vw // LANES)),
            pl.BlockSpec((1, LANES), lambda i: (0, 0)),
            pl.BlockSpec((1, vw), lambda i: (0, 0)),
        ],
        out_specs=pl.BlockSpec((L, vw), lambda i: (i, 0)),
        out_shape=jax.ShapeDtypeStruct((s, vw), BF16),
        scratch_shapes=[pltpu.VMEM((M_QK, M_V + LANES), F32)] * M_HEADS + [pltpu.VMEM((SUBLANES, LANES), F32)] * M_HEADS,
        compiler_params=_params("arbitrary"),
        name="mlstm",
    )(qkv, qkv, qkv, og, og, gate_bias, norm_g)


def _rot_cols(w):
    half = QK_ROPE // 2
    return jnp.concatenate([-w[:, half:], w[:, :half]], axis=1)


def _place(w, start):
    return jnp.pad(w, ((0, 0), (start, HEAD_PAD - start - w.shape[1])))


def _block_diag(w):
    g, bi, bo = w.shape
    eye = jnp.eye(g, dtype=w.dtype)
    return (w[:, :, None, :] * eye[:, None, :, None]).reshape(g * bi, g * bo)


def kernel(x, positions, e_norm_g, e_w_in, e_lru_conv_w, e_lru_conv_b, e_lru_w_a, e_lru_b_a, e_lru_w_x,
           e_lru_b_x, e_lru_lambda, e_q_norm_g, e_w_qb, e_kv_norm_g, e_w_kvb, e_w_out, o_norm_g, o_w_in,
           o_b_igate, o_b_fgate, o_out_norm_g, o_w_out, f_norm_g, f_w_up, f_conv_w, f_conv_b, f_w_down,
           final_norm_g):
    bsz, s, d = x.shape
    assert bsz == 1 and d == D_MODEL
    h = x.reshape(s, d)
    pos = positions.reshape(1, s)
    depth = f_norm_g.shape[0]

    half = QK_ROPE // 2
    inv_freq = ROPE_THETA ** (-jnp.arange(half, dtype=F32) / half)
    freq = inv_freq.reshape(half, 1)
    f_w_up_b = f_w_up.astype(BF16)
    f_w_down_b = f_w_down.astype(BF16)

    for layer in range(depth):
        j = layer // 2
        if layer % 2 == 0:
            w_in = e_w_in[j]
            c_kr = 2 * LRU_WIDTH + Q_LORA + KV_LORA
            w_kr = w_in[:, c_kr:c_kr + QK_ROPE]
            w_kr_placed = jnp.concatenate([_place(w_kr, QK_NOPE), _place(_rot_cols(w_kr), QK_NOPE)], axis=1)
            z, zkr = _norm_matmul(h, e_norm_g[j], [w_in[:, :c_kr].astype(BF16), w_kr_placed.astype(BF16)],
                                  (F32, F32), "even_in")

            y_lru = _rglru(
                z, e_lru_conv_w[j], e_lru_conv_b[j][None, :],
                _block_diag(e_lru_w_a[j]).astype(BF16), e_lru_b_a[j].reshape(1, LRU_WIDTH),
                _block_diag(e_lru_w_x[j]).astype(BF16), e_lru_b_x[j].reshape(1, LRU_WIDTH),
                e_lru_lambda[j][None, :])

            wq = e_w_qb[j].reshape(Q_LORA, MLA_HEADS, QK_NOPE + QK_ROPE)
            wq_a = jnp.pad(wq, ((0, 0), (0, 0), (0, HEAD_PAD - QK_NOPE - QK_ROPE)))
            wq_pe = wq[:, :, QK_NOPE:]
            wq_rot = jnp.concatenate([-wq_pe[:, :, half:], wq_pe[:, :, :half]], axis=2)
            wq_b = jnp.pad(wq_rot, ((0, 0), (0, 0), (QK_NOPE, HEAD_PAD - QK_NOPE - QK_ROPE)))
            wkv = e_w_kvb[j].reshape(KV_LORA, MLA_HEADS, QK_NOPE + V_HEAD)
            wk = jnp.pad(wkv[:, :, :QK_NOPE], ((0, 0), (0, 0), (0, HEAD_PAD - QK_NOPE)))
            hw = MLA_HEADS * HEAD_PAD
            qt, k, vt = _mla_prep(
                z, zkr, pos, e_q_norm_g[j][None, :], e_kv_norm_g[j][None, :],
                wq_a.reshape(Q_LORA, hw).T.astype(BF16), wq_b.reshape(Q_LORA, hw).T.astype(BF16),
                wk.reshape(KV_LORA, hw).astype(BF16),
                wkv[:, :, QK_NOPE:].reshape(KV_LORA, MLA_HEADS * V_HEAD).T.astype(BF16), freq)
            y_mla_t = _attention(qt, k, vt)

            w_out = e_w_out[j].astype(BF16)
            mix = ([y_lru, y_mla_t], [w_out[:LRU_WIDTH], w_out[LRU_WIDTH:]], (False, True))
        else:
            w_in = o_w_in[j]
            w_og = jnp.pad(w_in[:, ODD_QKV_COLS:], ((0, 0), (0, ODD_COLS - w_in.shape[1])))
            qkv, og = _norm_matmul(h, o_norm_g[j], [w_in[:, :ODD_QKV_COLS].astype(BF16), w_og.astype(BF16)],
                                   (BF16, F32), "odd_in")
            gate_bias = jnp.pad(jnp.concatenate([o_b_igate[j], o_b_fgate[j]])[None, :],
                                ((0, 0), (0, LANES - 2 * M_HEADS)))
            y = _mlstm(qkv, og, gate_bias, o_out_norm_g[j][None, :])
            mix = ([y], [o_w_out[j].astype(BF16)], (False,))

        h = _ffn(h, *mix, layer, f_norm_g[:, None, :], f_w_up_b, f_conv_w, f_conv_b[:, None, :], f_w_down_b,
                 final_norm_g[None, :], final_norm=(layer == depth - 1))
    return h.reshape(bsz, s, d)
```

```python
import collections
import functools

import jax
import jax.numpy as jnp
from jax import lax
from jax.experimental import pallas as pl
from jax.experimental.pallas import tpu as pltpu

F32 = jnp.float32
BF16 = jnp.bfloat16

EPS = 1e-6
LANES = 128
SUBLANES = 8
VMEM_LIMIT = 56 * 1024 * 1024

D_MODEL = 1024
LRU_WIDTH = 512
LRU_BLOCKS = 8
LRU_CONV = 4
LRU_C = 8.0
MLA_HEADS = 8
Q_LORA = 256
KV_LORA = 128
QK_NOPE = 64
QK_ROPE = 32
V_HEAD = 64
V_AUG = 80
ROPE_THETA = 10000.0
HEAD_PAD = 128
M_HEADS = 4
M_QK = 128
M_V = 256
GATE_CAP = 15.0
D_FF = 2816
FFN_CONV = 3

EVEN_COLS = 2 * LRU_WIDTH + Q_LORA + KV_LORA + 2 * HEAD_PAD
ODD_QKV_COLS = 2 * M_HEADS * M_QK + M_HEADS * M_V
ODD_COLS = ODD_QKV_COLS + M_HEADS * M_V + LANES

ROW_BLOCK = 512
ATTN_Q_BLOCK = 1024
ATTN_K_BLOCK = 256
ATTN_ROWS = 32
ATTN_HEADS = 2
ATTN_LANE_SPLIT = 4
ATTN_PV_LAG = 1
LOG2E = 1.4426950408889634
M_CHUNK = 256
FFN_CHUNK = 256
FFN_ROW_SPLIT = 2


def _params(*sem):
    return pltpu.CompilerParams(dimension_semantics=sem, vmem_limit_bytes=VMEM_LIMIT)


def _rms(x, g):
    inv = lax.rsqrt(jnp.mean(x * x, axis=-1, keepdims=True) + EPS)
    return x * inv * g


def _fold_rows(x, op):
    r = x.shape[0]
    while r > SUBLANES:
        r //= 2
        x = op(x[:r], x[r:])
    return x


def _shift_rows(x, prev_tail, sh):
    t, w = x.shape
    x3 = jnp.concatenate([prev_tail, x], axis=0).reshape(t // SUBLANES + 1, SUBLANES, w)
    rolled = pltpu.roll(x3, sh, axis=1)
    sub = lax.broadcasted_iota(jnp.int32, (t // SUBLANES, SUBLANES, w), 1)
    return jnp.where(sub >= sh, rolled[1:], rolled[:-1]).reshape(t, w)


def _norm_matmul_body(x_ref, g_ref, *refs):
    w_refs, o_refs = refs[:len(refs) // 2], refs[len(refs) // 2:]
    xn = _rms(x_ref[...], g_ref[...]).astype(BF16)
    for w_ref, o_ref in zip(w_refs, o_refs):
        o_ref[...] = jnp.dot(xn, w_ref[...], preferred_element_type=F32).astype(o_ref.dtype)


def _norm_matmul(x, g, ws, out_dtypes, name):
    s, d = x.shape
    tm = min(ROW_BLOCK, s)
    return pl.pallas_call(
        _norm_matmul_body,
        grid=(s // tm,),
        in_specs=[pl.BlockSpec((tm, d), lambda i: (i, 0)), pl.BlockSpec((1, d), lambda i: (0, 0))]
        + [pl.BlockSpec(w.shape, lambda i: (0, 0)) for w in ws],
        out_specs=[pl.BlockSpec((tm, w.shape[1]), lambda i: (i, 0)) for w in ws],
        out_shape=[jax.ShapeDtypeStruct((s, w.shape[1]), dt) for w, dt in zip(ws, out_dtypes)],
        compiler_params=_params("parallel"),
        name=name,
    )(x, g.reshape(1, d), *ws)


def _rglru_body(xr_ref, gate_ref, cw_ref, cb_ref, wa_ref, ba_ref, wx_ref, bx_ref, lam_ref,
                o_ref, tail_s, h_s, a_s, b_s):
    @pl.when(pl.program_id(0) == 0)
    def _():
        tail_s[...] = jnp.zeros_like(tail_s)
        h_s[...] = jnp.zeros_like(h_s)

    tm = xr_ref.shape[0]
    xr = xr_ref[...]
    tail = tail_s[...]
    tail_s[...] = xr[tm - SUBLANES:, :]
    xc = cb_ref[...] + _shift_rows(xr, tail, 3) * cw_ref[0:1, :]
    xc = xc + _shift_rows(xr, tail, 2) * cw_ref[1:2, :]
    xc = xc + _shift_rows(xr, tail, 1) * cw_ref[2:3, :]
    xc = xc + xr * cw_ref[3:4, :]

    xcb = xc.astype(BF16)
    r = jax.nn.sigmoid(jnp.dot(xcb, wa_ref[...], preferred_element_type=F32) + ba_ref[...])
    i = jax.nn.sigmoid(jnp.dot(xcb, wx_ref[...], preferred_element_type=F32) + bx_ref[...])
    neg_lam = -lam_ref[...]
    softplus = jnp.maximum(neg_lam, 0.0) + jnp.log1p(jnp.exp(-jnp.abs(neg_lam)))
    log_a = (-LRU_C * r) * softplus
    a = jnp.exp(log_a)
    th = jnp.tanh(log_a)
    y = -2.0 * th
    root = jnp.where(y > 0.0, y * lax.rsqrt(y), 0.0)
    u = root * lax.rsqrt(1.0 - th) * (i * xc)

    w = a.shape[1]
    a = a.reshape(tm // SUBLANES, SUBLANES, w)
    u = u.reshape(tm // SUBLANES, SUBLANES, w)
    sub = lax.broadcasted_iota(jnp.int32, a.shape, 1)
    for d in (1, 2, 4):
        keep = sub >= d
        a_sh = jnp.where(keep, pltpu.roll(a, d, axis=1), 1.0)
        u_sh = jnp.where(keep, pltpu.roll(u, d, axis=1), 0.0)
        u = a * u_sh + u
        a = a * a_sh
    a_s[...] = a.reshape(tm, w)
    b_s[...] = u.reshape(tm, w)

    def group(k, h):
        off = pl.multiple_of(k * SUBLANES, SUBLANES)
        ht = a_s[pl.ds(off, SUBLANES), :] * h + b_s[pl.ds(off, SUBLANES), :]
        b_s[pl.ds(off, SUBLANES), :] = ht
        return ht[SUBLANES - 1:, :]

    h_s[...] = lax.fori_loop(0, tm // SUBLANES, group, h_s[...], unroll=8)
    o_ref[...] = (b_s[...] * jax.nn.gelu(gate_ref[...])).astype(o_ref.dtype)


def _rglru(z, cw, cb, wa, ba, wx, bx, lam):
    s = z.shape[0]
    w = LRU_WIDTH
    tm = min(ROW_BLOCK, s)
    vec = lambda r: pl.BlockSpec((r, w), lambda i: (0, 0))
    return pl.pallas_call(
        _rglru_body,
        grid=(s // tm,),
        in_specs=[
            pl.BlockSpec((tm, w), lambda i: (i, 0)),
            pl.BlockSpec((tm, w), lambda i: (i, 1)),
            vec(LRU_CONV), vec(1), vec(w), vec(1), vec(w), vec(1), vec(1),
        ],
        out_specs=pl.BlockSpec((tm, w), lambda i: (i, 0)),
        out_shape=jax.ShapeDtypeStruct((s, w), BF16),
        scratch_shapes=[
            pltpu.VMEM((SUBLANES, w), F32),
            pltpu.VMEM((1, w), F32),
            pltpu.VMEM((tm, w), F32),
            pltpu.VMEM((tm, w), F32),
        ],
        compiler_params=_params("arbitrary"),
        name="rglru",
    )(z, z, cw, cb, wa, ba, wx, bx, lam)


def _mla_prep_body(zq_ref, zkv_ref, zkr_ref, zkrot_ref, pos_ref, qg_ref, kvg_ref,
                   wqa_ref, wqb_ref, wk_ref, wv_ref, freq_ref, qt_ref, k_ref, vt_ref):
    nt = (((1,), (1,)), ((), ()))
    tm = pos_ref.shape[1]
    ang = freq_ref[...] * pos_ref[...].astype(F32)
    cos_h, sin_h = jnp.cos(ang), jnp.sin(ang)
    zeros = lambda r: jnp.zeros((r, tm), F32)
    cos_t = jnp.concatenate([jnp.ones((QK_NOPE, tm), F32), cos_h, cos_h, zeros(HEAD_PAD - QK_NOPE - QK_ROPE)], axis=0)
    sin_t = jnp.concatenate([zeros(QK_NOPE), sin_h, sin_h, zeros(HEAD_PAD - QK_NOPE - QK_ROPE)], axis=0)
    scale = (QK_NOPE + QK_ROPE) ** -0.5 * LOG2E
    cq_t = cos_t * scale
    sq_t = sin_t * scale
    ck = cos_t.T
    sinv = sin_t.T

    qn = _rms(zq_ref[...], qg_ref[...]).astype(BF16)
    qa_t = lax.dot_general(wqa_ref[...], qn, nt, preferred_element_type=F32)
    qb_t = lax.dot_general(wqb_ref[...], qn, nt, preferred_element_type=F32)
    kvn = _rms(zkv_ref[...], kvg_ref[...]).astype(BF16)
    kn = jnp.dot(kvn, wk_ref[...], preferred_element_type=F32)
    v_t = lax.dot_general(wv_ref[...], kvn, nt, preferred_element_type=F32).astype(BF16)
    tk = vt_ref.shape[2]
    ones = jnp.ones((V_AUG - V_HEAD, tk), BF16)
    for c in range(vt_ref.shape[0]):
        for h in range(MLA_HEADS):
            vt_ref[c, h * V_AUG:h * V_AUG + V_HEAD, :] = v_t[h * V_HEAD:(h + 1) * V_HEAD, c * tk:(c + 1) * tk]
            vt_ref[c, h * V_AUG + V_HEAD:(h + 1) * V_AUG, :] = ones
    kpe = zkr_ref[...] * ck + zkrot_ref[...] * sinv
    for h in range(MLA_HEADS):
        hs = slice(h * HEAD_PAD, (h + 1) * HEAD_PAD)
        qt_ref[0, hs, :] = (qa_t[hs, :] * cq_t + qb_t[hs, :] * sq_t).astype(BF16)
        k_ref[:, hs] = (kn[:, hs] + kpe).astype(BF16)


def _mla_prep(z, zkr, pos, qg, kvg, wqa_t, wqb_t, wk, wv_t, freq):
    s = z.shape[0]
    tm = min(ATTN_Q_BLOCK, s)
    tk = ATTN_K_BLOCK
    assert s % tm == 0 and tm % (2 * tk) == 0
    nb = s // tm
    hw = MLA_HEADS * HEAD_PAD
    vw = MLA_HEADS * V_AUG
    c0 = 2 * LRU_WIDTH
    full = lambda a: pl.BlockSpec(a.shape, lambda i: (0, 0))
    return pl.pallas_call(
        _mla_prep_body,
        grid=(nb,),
        in_specs=[
            pl.BlockSpec((tm, Q_LORA), lambda i: (i, c0 // Q_LORA)),
            pl.BlockSpec((tm, KV_LORA), lambda i: (i, (c0 + Q_LORA) // KV_LORA)),
            pl.BlockSpec((tm, HEAD_PAD), lambda i: (i, 0)),
            pl.BlockSpec((tm, HEAD_PAD), lambda i: (i, 1)),
            pl.BlockSpec((1, tm), lambda i: (0, i)),
            full(qg), full(kvg), full(wqa_t), full(wqb_t), full(wk), full(wv_t), full(freq),
        ],
        out_specs=[
            pl.BlockSpec((1, hw, tm), lambda i: (i, 0, 0)),
            pl.BlockSpec((tm, hw), lambda i: (i, 0)),
            pl.BlockSpec((tm // tk, vw, tk), lambda i: (i, 0, 0)),
        ],
        out_shape=[
            jax.ShapeDtypeStruct((nb, hw, tm), BF16),
            jax.ShapeDtypeStruct((s, hw), BF16),
            jax.ShapeDtypeStruct((s // tk, vw, tk), BF16),
        ],
        compiler_params=_params("parallel"),
        name="mla_prep",
    )(z, z, zkr, zkr, pos, qg, kvg, wqa_t, wqb_t, wk, wv_t, freq)


_AttnHead = collections.namedtuple("_AttnHead", "s_buf smax_buf p_buf acc qt kcols vrows orows")


def _attn_body(qt_ref, k_ref, vt_ref, o_ref, *scratch):
    ns = ATTN_Q_BLOCK // ATTN_K_BLOCK
    ahead = ns - 2
    lag = ATTN_PV_LAG
    npb = lag + 1
    per_head = 2 * ns + npb + 1
    tq = qt_ref.shape[2]
    tk = scratch[0].shape[0]
    i = pl.program_id(1)
    neg = jnp.finfo(F32).min

    def head(hh):
        sc = scratch[hh * per_head:(hh + 1) * per_head]
        return _AttnHead(
            s_buf=sc[:ns], smax_buf=sc[ns:2 * ns], p_buf=sc[2 * ns:2 * ns + npb], acc=sc[-1],
            qt=qt_ref[0, hh * HEAD_PAD:(hh + 1) * HEAD_PAD, :],
            kcols=slice(hh * HEAD_PAD, (hh + 1) * HEAD_PAD), vrows=slice(hh * V_AUG, (hh + 1) * V_AUG),
            orows=slice(hh * V_HEAD, (hh + 1) * V_HEAD))

    heads = [head(hh) for hh in range(ATTN_HEADS)]

    def qk(hd, b, c, lo=0, hi=None):
        off = pl.multiple_of(b * tk, tk)
        s = jnp.dot(k_ref[pl.ds(off, tk), hd.kcols], hd.qt[:, lo:hi], preferred_element_type=F32)
        hd.s_buf[c % ns][:, lo:hi] = s
        hd.smax_buf[c % ns][:, lo:hi] = _fold_rows(s, jnp.maximum)

    def softmax(hd, c, m, lane0=None):
        diag = lane0 is not None
        lane0 = lane0 or 0
        sb = hd.s_buf[c % ns]
        p_buf = hd.p_buf[c % npb]
        chunks = range(0, tk, ATTN_ROWS)
        rest = slice(lane0 + tk, tq)
        join = lambda parts: jnp.concatenate(parts, axis=1) if len(parts) > 1 else parts[0]

        def tri(r):
            s = sb[r:r + ATTN_ROWS, lane0:lane0 + tk]
            key = lax.broadcasted_iota(jnp.int32, s.shape, 0) + r
            return jnp.where(key <= lax.broadcasted_iota(jnp.int32, s.shape, 1), s, neg)

        def load(r):
            if not diag:
                return sb[r:r + ATTN_ROWS, :]
            return join([tri(r)] + ([sb[r:r + ATTN_ROWS, rest]] if rest.start < tq else []))

        if not diag:
            smax = hd.smax_buf[c % ns][...]
        else:
            smax_tri = functools.reduce(jnp.maximum, [_fold_rows(tri(r), jnp.maximum) for r in chunks])
            smax = join([smax_tri] + ([hd.smax_buf[c % ns][:, rest]] if rest.start < tq else []))
        m_old = m[:, lane0:]
        m_new = jnp.maximum(m_old, jnp.max(smax, axis=0, keepdims=True))
        for r in chunks:
            p_buf[r:r + ATTN_ROWS, lane0:] = jnp.exp2(load(r) - m_new).astype(BF16)
        alpha = jnp.exp2(m_old - m_new)
        if lane0:
            p_buf[:, :lane0] = jnp.zeros((tk, lane0), BF16)
            m_new = jnp.concatenate([m[:, :lane0], m_new], axis=1)
            alpha = jnp.concatenate([jnp.ones((1, lane0), F32), alpha], axis=1)
        return m_new, alpha

    def pv(hd, b, c, alpha, lo=0, hi=None):
        hd.acc[:, lo:hi] = alpha[:, lo:hi] * hd.acc[:, lo:hi] + jnp.dot(
            vt_ref[b, hd.vrows, :], hd.p_buf[c % npb][:, lo:hi], preferred_element_type=F32)

    def step(hd, state, b, c, diag):
        m, alphas = state
        b_pv = jnp.maximum(b - lag, 0)
        if not diag:
            part = tq // ATTN_LANE_SPLIT
            for lo in range(0, tq, part):
                qk(hd, b + ahead, c + ahead, lo, lo + part)
                pv(hd, b_pv, c - lag, alphas[0], lo, lo + part)
            m, alpha = softmax(hd, c, m)
            return m, alphas[1:] + (alpha,)
        if c + ahead < ns:
            qk(hd, b + ahead, c + ahead, (c + ahead) * tk)
        pv(hd, b_pv, c - lag, alphas[0])
        m, alpha = softmax(hd, c, m, c * tk if diag else None)
        return m, alphas[1:] + (alpha,)

    def trip(t, carry):
        carry = list(carry)
        for c in range(ns):
            for hh, hd in enumerate(heads):
                carry[hh] = step(hd, carry[hh], t * ns + c, c, diag=False)
        return tuple(carry)

    for hd in heads:
        for c in range(ahead):
            qk(hd, c, c)
        for k in range(1, lag + 1):
            hd.p_buf[-k % npb][...] = jnp.zeros_like(hd.p_buf[-k % npb])
        hd.acc[...] = jnp.zeros_like(hd.acc)
    init = (jnp.full((1, tq), -jnp.inf, F32), (jnp.ones((1, tq), F32),) * lag)
    carry = list(lax.fori_loop(0, i, trip, (init,) * ATTN_HEADS))

    b0 = i * ns
    for c in range(ns):
        for hh, hd in enumerate(heads):
            carry[hh] = step(hd, carry[hh], b0 + c, c, diag=True)
    for hh, hd in enumerate(heads):
        for k in range(lag):
            pv(hd, b0 + ns - lag + k, ns - lag + k, carry[hh][1][k])
        o_ref[hd.orows, :] = (hd.acc[:V_HEAD, :] / hd.acc[V_HEAD:V_HEAD + 1, :]).astype(o_ref.dtype)


def _attention(qt, k, vt):
    nb, hw, tq = qt.shape
    nkb, vw, tk = vt.shape
    s = k.shape[0]
    hp = ATTN_HEADS
    per_head = ([pltpu.VMEM((tk, tq), F32)] * (tq // tk) + [pltpu.VMEM((SUBLANES, tq), F32)] * (tq // tk)
                + [pltpu.VMEM((tk, tq), BF16)] * (ATTN_PV_LAG + 1) + [pltpu.VMEM((V_AUG, tq), F32)])
    return pl.pallas_call(
        _attn_body,
        grid=(MLA_HEADS // hp, nb),
        in_specs=[
            pl.BlockSpec((1, hp * HEAD_PAD, tq), lambda g, i: (i, g, 0)),
            pl.BlockSpec((s, hp * HEAD_PAD), lambda g, i: (0, g)),
            pl.BlockSpec((nkb, hp * V_AUG, tk), lambda g, i: (0, g, 0)),
        ],
        out_specs=pl.BlockSpec((hp * V_HEAD, tq), lambda g, i: (g, i)),
        out_shape=jax.ShapeDtypeStruct((MLA_HEADS * V_HEAD, s), BF16),
        scratch_shapes=per_head * hp,
        compiler_params=_params("parallel", "parallel"),
        name="attention",
    )(qt, k, vt)


def _ffn_body(*refs, transposed, final_norm):
    n_mix = len(transposed)
    res_ref = refs[0]
    y_refs, w_refs = refs[1:1 + n_mix], refs[1 + n_mix:1 + 2 * n_mix]
    g_ref, wup_ref, cw_ref, cb_ref, wdn_ref, fg_ref, o_ref, tail_s, act_s, xn_s = refs[1 + 2 * n_mix:]

    @pl.when(pl.program_id(0) == 0)
    def _():
        tail_s[...] = jnp.zeros_like(tail_s)

    tm = res_ref.shape[0]
    x = res_ref[...]
    for y_ref, w_ref, tr in zip(y_refs, w_refs, transposed):
        dims = (((0,), (0,)), ((), ())) if tr else (((1,), (0,)), ((), ()))
        x = x + lax.dot_general(y_ref[...], w_ref[...], dims, preferred_element_type=F32)
    xn_s[...] = _rms(x, g_ref[...]).astype(BF16)

    def conv(rows, col, tail):
        cs = pl.ds(col, FFN_CHUNK)
        u = jnp.dot(xn_s[rows, :], wup_ref[:, cs], preferred_element_type=F32)
        y = cb_ref[:, cs] + _shift_rows(u, tail, 2) * cw_ref[0:1, cs]
        y = y + _shift_rows(u, tail, 1) * cw_ref[1:2, cs]
        return y + u * cw_ref[2:3, cs], u[u.shape[0] - SUBLANES:, :]

    sub = tm // FFN_ROW_SPLIT
    for c in range(D_FF // FFN_CHUNK):
        gcol, vcol = c * FFN_CHUNK, D_FF + c * FFN_CHUNK
        g_tail, v_tail = tail_s[:, pl.ds(gcol, FFN_CHUNK)], tail_s[:, pl.ds(vcol, FFN_CHUNK)]
        for r in range(FFN_ROW_SPLIT):
            rows = pl.ds(r * sub, sub)
            gate, g_tail = conv(rows, gcol, g_tail)
            val, v_tail = conv(rows, vcol, v_tail)
            act_s[rows, pl.ds(c * FFN_CHUNK, FFN_CHUNK)] = (jax.nn.silu(gate) * val).astype(BF16)
        tail_s[:, pl.ds(gcol, FFN_CHUNK)] = g_tail
        tail_s[:, pl.ds(vcol, FFN_CHUNK)] = v_tail

    out = x + jnp.dot(act_s[...], wdn_ref[...], preferred_element_type=F32)
    if final_norm:
        out = _rms(out, fg_ref[...])
    o_ref[...] = out


def _ffn(res, ys, ws, transposed, layer, g, wup, cw, cb, wdn, fg, final_norm):
    s, d = res.shape
    tm = min(ROW_BLOCK, s)
    const = lambda a: pl.BlockSpec(a.shape, lambda i: (0, 0), pipeline_mode=pl.Buffered(1))
    stacked = lambda a: pl.BlockSpec((None,) + a.shape[1:], lambda i: (layer, 0, 0), pipeline_mode=pl.Buffered(1))
    y_specs = [pl.BlockSpec((y.shape[0], tm), lambda i: (0, i)) if tr else
               pl.BlockSpec((tm, y.shape[1]), lambda i: (i, 0)) for y, tr in zip(ys, transposed)]
    return pl.pallas_call(
        functools.partial(_ffn_body, transposed=transposed, final_norm=final_norm),
        grid=(s // tm,),
        in_specs=[pl.BlockSpec((tm, d), lambda i: (i, 0))] + y_specs + [const(w) for w in ws]
        + [stacked(g), stacked(wup), stacked(cw), stacked(cb), stacked(wdn), const(fg)],
        out_specs=pl.BlockSpec((tm, d), lambda i: (i, 0)),
        out_shape=jax.ShapeDtypeStruct((s, d), F32),
        scratch_shapes=[pltpu.VMEM((SUBLANES, 2 * D_FF), F32), pltpu.VMEM((tm, D_FF), BF16),
                        pltpu.VMEM((tm, d), BF16)],
        compiler_params=_params("arbitrary"),
        name="ffn_final" if final_norm else "ffn",
    )(res, *ys, *ws, g, wup, cw, cb, wdn, fg)


def _mlstm_body(q_ref, k_ref, v_ref, o_ref, gate_ref, gb_ref, ng_ref, y_ref, *state):
    c_s, m_s = state[:M_HEADS], state[M_HEADS:]

    @pl.when(pl.program_id(0) == 0)
    def _():
        for ref in state:
            ref[...] = jnp.zeros_like(ref)

    L = q_ref.shape[0]
    capped = GATE_CAP * jnp.tanh((gate_ref[...] + gb_ref[...]) / GATE_CAP)
    log_f = jax.nn.log_sigmoid(capped)
    row = lax.broadcasted_iota(jnp.int32, (L, LANES), 0)
    b_all = log_f
    d = 1
    while d < L:
        b_all = b_all + jnp.where(row >= d, pltpu.roll(b_all, d, axis=0), 0.0)
        d *= 2
    i_t = capped.T
    b_t = b_all.T

    tri = lax.broadcasted_iota(jnp.int32, (L, L), 1) <= lax.broadcasted_iota(jnp.int32, (L, L), 0)
    ones = jnp.ones((L, LANES), BF16)
    q_scale = M_QK ** -0.5
    heads = range(M_HEADS)
    qs = [q_ref[:, h * M_QK:(h + 1) * M_QK] for h in heads]
    ks = [k_ref[:, h * M_QK:(h + 1) * M_QK] for h in heads]
    vaugs = [jnp.concatenate([v_ref[:, h * M_V:(h + 1) * M_V], ones], axis=1) for h in heads]
    c_sts = [c_s[h][...] for h in heads]
    m_sts = [m_s[h][0:1, :] for h in heads]
    qks = [lax.dot_general(qs[h], ks[h], (((1,), (1,)), ((), ())), preferred_element_type=F32) for h in heads]
    qcs = [jnp.dot(qs[h], c_sts[h].astype(BF16), preferred_element_type=F32) for h in heads]

    lanes = lambda col: jnp.broadcast_to(col, (L, LANES))
    wide = lambda a, n: jnp.concatenate([a] * n, axis=1)

    gate = []
    for h in heads:
        b_c = lanes(b_all[:, M_HEADS + h:M_HEADS + h + 1])
        i_c = lanes(capped[:, h:h + 1])
        b_row = b_t[M_HEADS + h:M_HEADS + h + 1, :]
        i_row = i_t[h:h + 1, :]
        dmat = jnp.where(tri, wide(b_c, L // LANES) - b_row + i_row, -jnp.inf)
        inter = b_c + m_sts[h]
        m_t = jnp.maximum(inter, lanes(jnp.max(dmat, axis=-1, keepdims=True)))
        w = jnp.exp(dmat - wide(m_t, L // LANES)) * q_scale
        g = jnp.exp(inter - m_t) * q_scale
        b_last = b_c[L - 1:, :]
        w_end = b_last - b_c + i_c
        m_new = jnp.maximum(b_last + m_sts[h], jnp.max(w_end, axis=0, keepdims=True))
        g_end = jnp.exp(b_last + m_sts[h] - m_new)
        gate.append((w, g, m_t, m_new, g_end, jnp.exp(w_end - m_new)))

    n_aug = (M_V + LANES) // LANES
    nums = []
    for h in heads:
        w, g, _, m_new, g_end, w_s = gate[h]
        sc = (qks[h] * w).astype(BF16)
        nums.append(wide(g, n_aug) * qcs[h] + jnp.dot(sc, vaugs[h], preferred_element_type=F32))
        kw = (ks[h].astype(F32) * w_s).astype(BF16)
        c_s[h][...] = wide(g_end, n_aug) * c_sts[h] + lax.dot_general(
            kw, vaugs[h], (((0,), (0,)), ((), ())), preferred_element_type=F32)
        m_s[h][...] = jnp.broadcast_to(m_new, m_s[h].shape)

    for h in heads:
        m_t = gate[h][2]
        den = nums[h][:, M_V:]
        hh = nums[h][:, :M_V] / wide(jnp.maximum(jnp.abs(den), jnp.exp(-m_t)), M_V // LANES)
        msq = lanes(jnp.mean(hh * hh, axis=-1, keepdims=True))
        hn = hh * wide(lax.rsqrt(msq + EPS), M_V // LANES)
        vs = slice(h * M_V, (h + 1) * M_V)
        y_ref[:, vs] = (hn * ng_ref[:, vs] * jax.nn.sigmoid(o_ref[:, vs])).astype(y_ref.dtype)


def _mlstm(qkv, og, gate_bias, norm_g):
    s = qkv.shape[0]
    L = min(M_CHUNK, s)
    qk = M_HEADS * M_QK
    vw = M_HEADS * M_V
    return pl.pallas_call(
        _mlstm_body,
        grid=(s // L,),
        in_specs=[
            pl.BlockSpec((L, qk), lambda i: (i, 0)),
            pl.BlockSpec((L, qk), lambda i: (i, 1)),
            pl.BlockSpec((L, vw), lambda i: (i, 2 * qk // vw)),
            pl.BlockSpec((L, vw), lambda i: (i, 0)),
            pl.BlockSpec((L, LANES), lambda i: (i, vw // LANES)),
            pl.BlockSpec((1, LANES), lambda i: (0, 0)),
            pl.BlockSpec((1, vw), lambda i: (0, 0)),
        ],
        out_specs=pl.BlockSpec((L, vw), lambda i: (i, 0)),
        out_shape=jax.ShapeDtypeStruct((s, vw), BF16),
        scratch_shapes=[pltpu.VMEM((M_QK, M_V + LANES), F32)] * M_HEADS + [pltpu.VMEM((SUBLANES, LANES), F32)] * M_HEADS,
        compiler_params=_params("arbitrary"),
        name="mlstm",
    )(qkv, qkv, qkv, og, og, gate_bias, norm_g)


def _rot_cols(w):
    half = QK_ROPE // 2
    return jnp.concatenate([-w[:, half:], w[:, :half]], axis=1)


def _place(w, start):
    return jnp.pad(w, ((0, 0), (start, HEAD_PAD - start - w.shape[1])))


def _block_diag(w):
    g, bi, bo = w.shape
    eye = jnp.eye(g, dtype=w.dtype)
    return (w[:, :, None, :] * eye[:, None, :, None]).reshape(g * bi, g * bo)


def kernel(x, positions, e_norm_g, e_w_in, e_lru_conv_w, e_lru_conv_b, e_lru_w_a, e_lru_b_a, e_lru_w_x,
           e_lru_b_x, e_lru_lambda, e_q_norm_g, e_w_qb, e_kv_norm_g, e_w_kvb, e_w_out, o_norm_g, o_w_in,
           o_b_igate, o_b_fgate, o_out_norm_g, o_w_out, f_norm_g, f_w_up, f_conv_w, f_conv_b, f_w_down,
           final_norm_g):
    bsz, s, d = x.shape
    assert bsz == 1 and d == D_MODEL
    h = x.reshape(s, d)
    pos = positions.reshape(1, s)
    depth = f_norm_g.shape[0]

    half = QK_ROPE // 2
    inv_freq = ROPE_THETA ** (-jnp.arange(half, dtype=F32) / half)
    freq = inv_freq.reshape(half, 1)
    f_w_up_b = f_w_up.astype(BF16)
    f_w_down_b = f_w_down.astype(BF16)

    for layer in range(depth):
        j = layer // 2
        if layer % 2 == 0:
            w_in = e_w_in[j]
            c_kr = 2 * LRU_WIDTH + Q_LORA + KV_LORA
            w_kr = w_in[:, c_kr:c_kr + QK_ROPE]
            w_kr_placed = jnp.concatenate([_place(w_kr, QK_NOPE), _place(_rot_cols(w_kr), QK_NOPE)], axis=1)
            z, zkr = _norm_matmul(h, e_norm_g[j], [w_in[:, :c_kr].astype(BF16), w_kr_placed.astype(BF16)],
                                  (F32, F32), "even_in")

            y_lru = _rglru(
                z, e_lru_conv_w[j], e_lru_conv_b[j][None, :],
                _block_diag(e_lru_w_a[j]).astype(BF16), e_lru_b_a[j].reshape(1, LRU_WIDTH),
                _block_diag(e_lru_w_x[j]).astype(BF16), e_lru_b_x[j].reshape(1, LRU_WIDTH),
                e_lru_lambda[j][None, :])

            wq = e_w_qb[j].reshape(Q_LORA, MLA_HEADS, QK_NOPE + QK_ROPE)
            wq_a = jnp.pad(wq, ((0, 0), (0, 0), (0, HEAD_PAD - QK_NOPE - QK_ROPE)))
            wq_pe = wq[:, :, QK_NOPE:]
            wq_rot = jnp.concatenate([-wq_pe[:, :, half:], wq_pe[:, :, :half]], axis=2)
            wq_b = jnp.pad(wq_rot, ((0, 0), (0, 0), (QK_NOPE, HEAD_PAD - QK_NOPE - QK_ROPE)))
            wkv = e_w_kvb[j].reshape(KV_LORA, MLA_HEADS, QK_NOPE + V_HEAD)
            wk = jnp.pad(wkv[:, :, :QK_NOPE], ((0, 0), (0, 0), (0, HEAD_PAD - QK_NOPE)))
            hw = MLA_HEADS * HEAD_PAD
            qt, k, vt = _mla_prep(
                z, zkr, pos, e_q_norm_g[j][None, :], e_kv_norm_g[j][None, :],
                wq_a.reshape(Q_LORA, hw).T.astype(BF16), wq_b.reshape(Q_LORA, hw).T.astype(BF16),
                wk.reshape(KV_LORA, hw).astype(BF16),
                wkv[:, :, QK_NOPE:].reshape(KV_LORA, MLA_HEADS * V_HEAD).T.astype(BF16), freq)
            y_mla_t = _attention(qt, k, vt)

            w_out = e_w_out[j].astype(BF16)
            mix = ([y_lru, y_mla_t], [w_out[:LRU_WIDTH], w_out[LRU_WIDTH:]], (False, True))
        else:
            w_in = o_w_in[j]
            w_og = jnp.pad(w_in[:, ODD_QKV_COLS:], ((0, 0), (0, ODD_COLS - w_in.shape[1])))
            qkv, og = _norm_matmul(h, o_norm_g[j], [w_in[:, :ODD_QKV_COLS].astype(BF16), w_og.astype(BF16)],
                                   (BF16, F32), "odd_in")
            gate_bias = jnp.pad(jnp.concatenate([o_b_igate[j], o_b_fgate[j]])[None, :],
                                ((0, 0), (0, LANES - 2 * M_HEADS)))
            y = _mlstm(qkv, og, gate_bias, o_out_norm_g[j][None, :])
            mix = ([y], [o_w_out[j].astype(BF16)], (False,))

        h = _ffn(h, *mix, layer, f_norm_g[:, None, :], f_w_up_b, f_conv_w, f_conv_b[:, None, :], f_w_down_b,
                 final_norm_g[None, :], final_norm=(layer == depth - 1))
    return h.reshape(bsz, s, d)
```

```python
import collections
import functools

import jax
import jax.numpy as jnp
from jax import lax
from jax.experimental import pallas as pl
from jax.experimental.pallas import tpu as pltpu

F32 = jnp.float32
BF16 = jnp.bfloat16

EPS = 1e-6
LANES = 128
SUBLANES = 8
VMEM_LIMIT = 56 * 1024 * 1024

D_MODEL = 1024
LRU_WIDTH = 512
LRU_CONV = 4
LRU_C = 8.0
MLA_HEADS = 8
Q_LORA = 256
KV_LORA = 128
QK_NOPE = 64
QK_ROPE = 32
V_HEAD = 64
V_AUG = 80
ROPE_THETA = 10000.0
HEAD_PAD = 128
M_HEADS = 4
M_QK = 128
M_V = 256
GATE_CAP = 15.0
D_FF = 2816

ODD_QKV_COLS = 2 * M_HEADS * M_QK + M_HEADS * M_V
ODD_COLS = ODD_QKV_COLS + M_HEADS * M_V + LANES

ROW_BLOCK = 512
ATTN_Q_BLOCK = 1024
ATTN_K_BLOCK = 256
ATTN_ROWS = 32
ATTN_HEADS = 2
ATTN_LANE_SPLIT = 4
ATTN_TRIPS = 2
ATTN_PV_LAG = 1
LOG2E = 1.4426950408889634
M_CHUNK = 256
M_CHUNKS_PER_STEP = 2
FFN_CHUNK = 256
FFN_ROW_SPLIT = 2


def _params(*sem):
    return pltpu.CompilerParams(dimension_semantics=sem, vmem_limit_bytes=VMEM_LIMIT)


def _rms(x, g):
    inv = lax.rsqrt(jnp.mean(x * x, axis=-1, keepdims=True) + EPS)
    return x * inv * g


def _fold_rows(x, op):
    r = x.shape[0]
    while r > SUBLANES:
        r //= 2
        x = op(x[:r], x[r:])
    return x


def _shift_rows(x, prev_tail, sh):
    t, w = x.shape
    x3 = jnp.concatenate([prev_tail, x], axis=0).reshape(t // SUBLANES + 1, SUBLANES, w)
    rolled = pltpu.roll(x3, sh, axis=1)
    sub = lax.broadcasted_iota(jnp.int32, (t // SUBLANES, SUBLANES, w), 1)
    return jnp.where(sub >= sh, rolled[1:], rolled[:-1]).reshape(t, w)


def _norm_matmul_body(x_ref, g_ref, *refs):
    w_refs, o_refs = refs[:len(refs) // 2], refs[len(refs) // 2:]
    xn = _rms(x_ref[...], g_ref[...]).astype(BF16)
    for w_ref, o_ref in zip(w_refs, o_refs):
        o_ref[...] = jnp.dot(xn, w_ref[...], preferred_element_type=F32).astype(o_ref.dtype)


def _norm_matmul(x, g, ws, out_dtypes, name):
    s, d = x.shape
    tm = min(ROW_BLOCK, s)
    return pl.pallas_call(
        _norm_matmul_body,
        grid=(s // tm,),
        in_specs=[pl.BlockSpec((tm, d), lambda i: (i, 0)), pl.BlockSpec((1, d), lambda i: (0, 0))]
        + [pl.BlockSpec(w.shape, lambda i: (0, 0)) for w in ws],
        out_specs=[pl.BlockSpec((tm, w.shape[1]), lambda i: (i, 0)) for w in ws],
        out_shape=[jax.ShapeDtypeStruct((s, w.shape[1]), dt) for w, dt in zip(ws, out_dtypes)],
        compiler_params=_params("parallel"),
        name=name,
    )(x, g.reshape(1, d), *ws)


def _rglru_body(xr_ref, gate_ref, cw_ref, cb_ref, wa_ref, ba_ref, wx_ref, bx_ref, lam_ref,
                o_ref, tail_s, h_s, a_s, b_s):
    @pl.when(pl.program_id(0) == 0)
    def _():
        tail_s[...] = jnp.zeros_like(tail_s)
        h_s[...] = jnp.zeros_like(h_s)

    tm = xr_ref.shape[0]
    xr = xr_ref[...]
    tail = tail_s[...]
    tail_s[...] = xr[tm - SUBLANES:, :]
    xc = cb_ref[...] + _shift_rows(xr, tail, 3) * cw_ref[0:1, :]
    xc = xc + _shift_rows(xr, tail, 2) * cw_ref[1:2, :]
    xc = xc + _shift_rows(xr, tail, 1) * cw_ref[2:3, :]
    xc = xc + xr * cw_ref[3:4, :]

    xcb = xc.astype(BF16)
    r = jax.nn.sigmoid(jnp.dot(xcb, wa_ref[...], preferred_element_type=F32) + ba_ref[...])
    i = jax.nn.sigmoid(jnp.dot(xcb, wx_ref[...], preferred_element_type=F32) + bx_ref[...])
    neg_lam = -lam_ref[...]
    softplus = jnp.maximum(neg_lam, 0.0) + jnp.log1p(jnp.exp(-jnp.abs(neg_lam)))
    log_a = (-LRU_C * r) * softplus
    a = jnp.exp(log_a)
    th = jnp.tanh(log_a)
    y = -2.0 * th
    root = jnp.where(y > 0.0, y * lax.rsqrt(y), 0.0)
    u = root * lax.rsqrt(1.0 - th) * (i * xc)

    w = a.shape[1]
    a = a.reshape(tm // SUBLANES, SUBLANES, w)
    u = u.reshape(tm // SUBLANES, SUBLANES, w)
    sub = lax.broadcasted_iota(jnp.int32, a.shape, 1)
    for d in (1, 2, 4):
        keep = sub >= d
        a_sh = jnp.where(keep, pltpu.roll(a, d, axis=1), 1.0)
        u_sh = jnp.where(keep, pltpu.roll(u, d, axis=1), 0.0)
        u = a * u_sh + u
        a = a * a_sh
    a_s[...] = a.reshape(tm, w)
    b_s[...] = u.reshape(tm, w)

    def group(k, h):
        off = pl.multiple_of(k * SUBLANES, SUBLANES)
        ht = a_s[pl.ds(off, SUBLANES), :] * h + b_s[pl.ds(off, SUBLANES), :]
        b_s[pl.ds(off, SUBLANES), :] = ht
        return ht[SUBLANES - 1:, :]

    h_s[...] = lax.fori_loop(0, tm // SUBLANES, group, h_s[...], unroll=8)
    o_ref[...] = (b_s[...] * jax.nn.gelu(gate_ref[...])).astype(o_ref.dtype)


def _rglru(z, cw, cb, wa, ba, wx, bx, lam):
    s = z.shape[0]
    w = LRU_WIDTH
    tm = min(ROW_BLOCK, s)
    vec = lambda r: pl.BlockSpec((r, w), lambda i: (0, 0))
    return pl.pallas_call(
        _rglru_body,
        grid=(s // tm,),
        in_specs=[
            pl.BlockSpec((tm, w), lambda i: (i, 0)),
            pl.BlockSpec((tm, w), lambda i: (i, 1)),
            vec(LRU_CONV), vec(1), vec(w), vec(1), vec(w), vec(1), vec(1),
        ],
        out_specs=pl.BlockSpec((tm, w), lambda i: (i, 0)),
        out_shape=jax.ShapeDtypeStruct((s, w), BF16),
        scratch_shapes=[
            pltpu.VMEM((SUBLANES, w), F32),
            pltpu.VMEM((1, w), F32),
            pltpu.VMEM((tm, w), F32),
            pltpu.VMEM((tm, w), F32),
        ],
        compiler_params=_params("arbitrary"),
        name="rglru",
    )(z, z, cw, cb, wa, ba, wx, bx, lam)


def _mla_prep_body(zq_ref, zkv_ref, zkr_ref, zkrot_ref, pos_ref, qg_ref, kvg_ref,
                   wqa_ref, wqb_ref, wk_ref, wv_ref, freq_ref, qt_ref, k_ref, vt_ref):
    nt = (((1,), (1,)), ((), ()))
    tm = pos_ref.shape[1]
    ang = freq_ref[...] * pos_ref[...].astype(F32)
    cos_h, sin_h = jnp.cos(ang), jnp.sin(ang)
    zeros = lambda r: jnp.zeros((r, tm), F32)
    cos_t = jnp.concatenate([jnp.ones((QK_NOPE, tm), F32), cos_h, cos_h, zeros(HEAD_PAD - QK_NOPE - QK_ROPE)], axis=0)
    sin_t = jnp.concatenate([zeros(QK_NOPE), sin_h, sin_h, zeros(HEAD_PAD - QK_NOPE - QK_ROPE)], axis=0)
    scale = (QK_NOPE + QK_ROPE) ** -0.5 * LOG2E
    cq_t = cos_t * scale
    sq_t = sin_t * scale
    ck = cos_t.T
    sinv = sin_t.T

    qn = _rms(zq_ref[...], qg_ref[...]).astype(BF16)
    qa_t = lax.dot_general(wqa_ref[...], qn, nt, preferred_element_type=F32)
    qb_t = lax.dot_general(wqb_ref[...], qn, nt, preferred_element_type=F32)
    kvn = _rms(zkv_ref[...], kvg_ref[...]).astype(BF16)
    kn = jnp.dot(kvn, wk_ref[...], preferred_element_type=F32)
    v_t = lax.dot_general(wv_ref[...], kvn, nt, preferred_element_type=F32).astype(BF16)
    tk = vt_ref.shape[2]
    ones = jnp.ones((V_AUG - V_HEAD, tk), BF16)
    for c in range(vt_ref.shape[0]):
        for h in range(MLA_HEADS):
            vt_ref[c, h * V_AUG:h * V_AUG + V_HEAD, :] = v_t[h * V_HEAD:(h + 1) * V_HEAD, c * tk:(c + 1) * tk]
            vt_ref[c, h * V_AUG + V_HEAD:(h + 1) * V_AUG, :] = ones
    kpe = zkr_ref[...] * ck + zkrot_ref[...] * sinv
    for h in range(MLA_HEADS):
        hs = slice(h * HEAD_PAD, (h + 1) * HEAD_PAD)
        qt_ref[0, hs, :] = (qa_t[hs, :] * cq_t + qb_t[hs, :] * sq_t).astype(BF16)
        k_ref[:, hs] = (kn[:, hs] + kpe).astype(BF16)


def _mla_prep(z, zkr, pos, qg, kvg, wqa_t, wqb_t, wk, wv_t, freq):
    s = z.shape[0]
    tm = min(ATTN_Q_BLOCK, s)
    tk = ATTN_K_BLOCK
    assert s % tm == 0 and tm % (2 * tk) == 0
    nb = s // tm
    hw = MLA_HEADS * HEAD_PAD
    vw = MLA_HEADS * V_AUG
    c0 = 2 * LRU_WIDTH
    full = lambda a: pl.BlockSpec(a.shape, lambda i: (0, 0))
    return pl.pallas_call(
        _mla_prep_body,
        grid=(nb,),
        in_specs=[
            pl.BlockSpec((tm, Q_LORA), lambda i: (i, c0 // Q_LORA)),
            pl.BlockSpec((tm, KV_LORA), lambda i: (i, (c0 + Q_LORA) // KV_LORA)),
            pl.BlockSpec((tm, HEAD_PAD), lambda i: (i, 0)),
            pl.BlockSpec((tm, HEAD_PAD), lambda i: (i, 1)),
            pl.BlockSpec((1, tm), lambda i: (0, i)),
            full(qg), full(kvg), full(wqa_t), full(wqb_t), full(wk), full(wv_t), full(freq),
        ],
        out_specs=[
            pl.BlockSpec((1, hw, tm), lambda i: (i, 0, 0)),
            pl.BlockSpec((tm, hw), lambda i: (i, 0)),
            pl.BlockSpec((tm // tk, vw, tk), lambda i: (i, 0, 0)),
        ],
        out_shape=[
            jax.ShapeDtypeStruct((nb, hw, tm), BF16),
            jax.ShapeDtypeStruct((s, hw), BF16),
            jax.ShapeDtypeStruct((s // tk, vw, tk), BF16),
        ],
        compiler_params=_params("parallel"),
        name="mla_prep",
    )(z, z, zkr, zkr, pos, qg, kvg, wqa_t, wqb_t, wk, wv_t, freq)


_AttnHead = collections.namedtuple("_AttnHead", "s_buf smax_buf p_buf acc qt kcols vrows orows")


def _attn_body(qt_ref, k_ref, vt_ref, o_ref, *scratch):
    ns = ATTN_Q_BLOCK // ATTN_K_BLOCK
    ahead = ns - 2
    lag = ATTN_PV_LAG
    npb = lag + 1
    per_head = 2 * ns + npb + 1
    tq = qt_ref.shape[2]
    tk = scratch[0].shape[0]
    i = pl.program_id(1)
    neg = jnp.finfo(F32).min

    def head(hh):
        sc = scratch[hh * per_head:(hh + 1) * per_head]
        return _AttnHead(
            s_buf=sc[:ns], smax_buf=sc[ns:2 * ns], p_buf=sc[2 * ns:2 * ns + npb], acc=sc[-1],
            qt=qt_ref[0, hh * HEAD_PAD:(hh + 1) * HEAD_PAD, :],
            kcols=slice(hh * HEAD_PAD, (hh + 1) * HEAD_PAD), vrows=slice(hh * V_AUG, (hh + 1) * V_AUG),
            orows=slice(hh * V_HEAD, (hh + 1) * V_HEAD))

    heads = [head(hh) for hh in range(ATTN_HEADS)]

    def qk(hd, b, c, lo=0, hi=None):
        off = pl.multiple_of(b * tk, tk)
        s = jnp.dot(k_ref[pl.ds(off, tk), hd.kcols], hd.qt[:, lo:hi], preferred_element_type=F32)
        hd.s_buf[c % ns][:, lo:hi] = s
        hd.smax_buf[c % ns][:, lo:hi] = _fold_rows(s, jnp.maximum)

    def softmax(hd, c, m, lane0=None):
        diag = lane0 is not None
        lane0 = lane0 or 0
        sb = hd.s_buf[c % ns]
        p_buf = hd.p_buf[c % npb]
        chunks = range(0, tk, ATTN_ROWS)
        rest = slice(lane0 + tk, tq)
        join = lambda parts: jnp.concatenate(parts, axis=1) if len(parts) > 1 else parts[0]

        def tri(r):
            s = sb[r:r + ATTN_ROWS, lane0:lane0 + tk]
            key = lax.broadcasted_iota(jnp.int32, s.shape, 0) + r
            return jnp.where(key <= lax.broadcasted_iota(jnp.int32, s.shape, 1), s, neg)

        def load(r):
            if not diag:
                return sb[r:r + ATTN_ROWS, :]
            return join([tri(r)] + ([sb[r:r + ATTN_ROWS, rest]] if rest.start < tq else []))

        if not diag:
            smax = hd.smax_buf[c % ns][...]
        else:
            smax_tri = functools.reduce(jnp.maximum, [_fold_rows(tri(r), jnp.maximum) for r in chunks])
            smax = join([smax_tri] + ([hd.smax_buf[c % ns][:, rest]] if rest.start < tq else []))
        m_old = m[:, lane0:]
        m_new = jnp.maximum(m_old, jnp.max(smax, axis=0, keepdims=True))
        for r in chunks:
            p_buf[r:r + ATTN_ROWS, lane0:] = jnp.exp2(load(r) - m_new).astype(BF16)
        alpha = jnp.exp2(m_old - m_new)
        if lane0:
            p_buf[:, :lane0] = jnp.zeros((tk, lane0), BF16)
            m_new = jnp.concatenate([m[:, :lane0], m_new], axis=1)
            alpha = jnp.concatenate([jnp.ones((1, lane0), F32), alpha], axis=1)
        return m_new, alpha

    def pv(hd, b, c, alpha, lo=0, hi=None):
        hd.acc[:, lo:hi] = alpha[:, lo:hi] * hd.acc[:, lo:hi] + jnp.dot(
            vt_ref[b, hd.vrows, :], hd.p_buf[c % npb][:, lo:hi], preferred_element_type=F32)

    def step(hd, state, b, c, diag):
        m, alphas = state
        b_pv = jnp.maximum(b - lag, 0)
        if not diag:
            part = tq // ATTN_LANE_SPLIT
            for lo in range(0, tq, part):
                qk(hd, b + ahead, c + ahead, lo, lo + part)
                pv(hd, b_pv, c - lag, alphas[0], lo, lo + part)
            m, alpha = softmax(hd, c, m)
            return m, alphas[1:] + (alpha,)
        if c + ahead < ns:
            qk(hd, b + ahead, c + ahead, (c + ahead) * tk)
        pv(hd, b_pv, c - lag, alphas[0])
        m, alpha = softmax(hd, c, m, c * tk if diag else None)
        return m, alphas[1:] + (alpha,)

    def trip(t, carry):
        carry = list(carry)
        for c in range(ns):
            for hh, hd in enumerate(heads):
                carry[hh] = step(hd, carry[hh], t * ns + c, c, diag=False)
        return tuple(carry)

    for hd in heads:
        for c in range(ahead):
            qk(hd, c, c)
        for k in range(1, lag + 1):
            hd.p_buf[-k % npb][...] = jnp.zeros_like(hd.p_buf[-k % npb])
        hd.acc[...] = jnp.zeros_like(hd.acc)
    def trips(t, carry):
        for k in range(ATTN_TRIPS):
            carry = trip(t * ATTN_TRIPS + k, carry)
        return carry

    init = (jnp.full((1, tq), -jnp.inf, F32), (jnp.ones((1, tq), F32),) * lag)
    carry = lax.fori_loop(0, i // ATTN_TRIPS, trips, (init,) * ATTN_HEADS)
    carry = list(lax.fori_loop(i // ATTN_TRIPS * ATTN_TRIPS, i, trip, carry))

    b0 = i * ns
    for c in range(ns):
        for hh, hd in enumerate(heads):
            carry[hh] = step(hd, carry[hh], b0 + c, c, diag=True)
    for hh, hd in enumerate(heads):
        for k in range(lag):
            pv(hd, b0 + ns - lag + k, ns - lag + k, carry[hh][1][k])
        o_ref[hd.orows, :] = (hd.acc[:V_HEAD, :] / hd.acc[V_HEAD:V_HEAD + 1, :]).astype(o_ref.dtype)


def _attention(qt, k, vt):
    nb, hw, tq = qt.shape
    nkb, vw, tk = vt.shape
    s = k.shape[0]
    hp = ATTN_HEADS
    per_head = ([pltpu.VMEM((tk, tq), F32)] * (tq // tk) + [pltpu.VMEM((SUBLANES, tq), F32)] * (tq // tk)
                + [pltpu.VMEM((tk, tq), BF16)] * (ATTN_PV_LAG + 1) + [pltpu.VMEM((V_AUG, tq), F32)])
    return pl.pallas_call(
        _attn_body,
        grid=(MLA_HEADS // hp, nb),
        in_specs=[
            pl.BlockSpec((1, hp * HEAD_PAD, tq), lambda g, i: (i, g, 0)),
            pl.BlockSpec((s, hp * HEAD_PAD), lambda g, i: (0, g)),
            pl.BlockSpec((nkb, hp * V_AUG, tk), lambda g, i: (0, g, 0)),
        ],
        out_specs=pl.BlockSpec((hp * V_HEAD, tq), lambda g, i: (g, i)),
        out_shape=jax.ShapeDtypeStruct((MLA_HEADS * V_HEAD, s), BF16),
        scratch_shapes=per_head * hp,
        compiler_params=_params("parallel", "parallel"),
        name="attention",
    )(qt, k, vt)


def _ffn_body(*refs, transposed, final_norm):
    n_mix = len(transposed)
    res_ref = refs[0]
    y_refs, w_refs = refs[1:1 + n_mix], refs[1 + n_mix:1 + 2 * n_mix]
    g_ref, wup_ref, cw_ref, cb_ref, wdn_ref, fg_ref, o_ref, tail_s, act_s, xn_s = refs[1 + 2 * n_mix:]

    @pl.when(pl.program_id(0) == 0)
    def _():
        tail_s[...] = jnp.zeros_like(tail_s)

    tm = res_ref.shape[0]
    x = res_ref[...]
    for y_ref, w_ref, tr in zip(y_refs, w_refs, transposed):
        dims = (((0,), (0,)), ((), ())) if tr else (((1,), (0,)), ((), ()))
        x = x + lax.dot_general(y_ref[...], w_ref[...], dims, preferred_element_type=F32)
    xn_s[...] = _rms(x, g_ref[...]).astype(BF16)

    def conv(rows, col, tail):
        cs = pl.ds(col, FFN_CHUNK)
        u = jnp.dot(xn_s[rows, :], wup_ref[:, cs], preferred_element_type=F32)
        y = cb_ref[:, cs] + _shift_rows(u, tail, 2) * cw_ref[0:1, cs]
        y = y + _shift_rows(u, tail, 1) * cw_ref[1:2, cs]
        return y + u * cw_ref[2:3, cs], u[u.shape[0] - SUBLANES:, :]

    sub = tm // FFN_ROW_SPLIT
    for c in range(D_FF // FFN_CHUNK):
        gcol, vcol = c * FFN_CHUNK, D_FF + c * FFN_CHUNK
        g_tail, v_tail = tail_s[:, pl.ds(gcol, FFN_CHUNK)], tail_s[:, pl.ds(vcol, FFN_CHUNK)]
        for r in range(FFN_ROW_SPLIT):
            rows = pl.ds(r * sub, sub)
            gate, g_tail = conv(rows, gcol, g_tail)
            val, v_tail = conv(rows, vcol, v_tail)
            act_s[rows, pl.ds(c * FFN_CHUNK, FFN_CHUNK)] = (jax.nn.silu(gate) * val).astype(BF16)
        tail_s[:, pl.ds(gcol, FFN_CHUNK)] = g_tail
        tail_s[:, pl.ds(vcol, FFN_CHUNK)] = v_tail

    out = x + jnp.dot(act_s[...], wdn_ref[...], preferred_element_type=F32)
    if final_norm:
        out = _rms(out, fg_ref[...])
    o_ref[...] = out


def _ffn(res, ys, ws, transposed, layer, g, wup, cw, cb, wdn, fg, final_norm):
    s, d = res.shape
    tm = min(ROW_BLOCK, s)
    const = lambda a: pl.BlockSpec(a.shape, lambda i: (0, 0), pipeline_mode=pl.Buffered(1))
    stacked = lambda a: pl.BlockSpec((None,) + a.shape[1:], lambda i: (layer, 0, 0), pipeline_mode=pl.Buffered(1))
    y_specs = [pl.BlockSpec((y.shape[0], tm), lambda i: (0, i)) if tr else
               pl.BlockSpec((tm, y.shape[1]), lambda i: (i, 0)) for y, tr in zip(ys, transposed)]
    return pl.pallas_call(
        functools.partial(_ffn_body, transposed=transposed, final_norm=final_norm),
        grid=(s // tm,),
        in_specs=[pl.BlockSpec((tm, d), lambda i: (i, 0))] + y_specs + [const(w) for w in ws]
        + [stacked(g), stacked(wup), stacked(cw), stacked(cb), stacked(wdn), const(fg)],
        out_specs=pl.BlockSpec((tm, d), lambda i: (i, 0)),
        out_shape=jax.ShapeDtypeStruct((s, d), F32),
        scratch_shapes=[pltpu.VMEM((SUBLANES, 2 * D_FF), F32), pltpu.VMEM((tm, D_FF), BF16),
                        pltpu.VMEM((tm, d), BF16)],
        compiler_params=_params("arbitrary"),
        name="ffn_final" if final_norm else "ffn",
    )(res, *ys, *ws, g, wup, cw, cb, wdn, fg)


def _mlstm_body(q_ref, k_ref, v_ref, o_ref, gate_ref, gb_ref, ng_ref, y_ref, *state):
    c_s, m_s = state[:M_HEADS], state[M_HEADS:]

    @pl.when(pl.program_id(0) == 0)
    def _():
        for ref in state:
            ref[...] = jnp.zeros_like(ref)

    for ci in range(q_ref.shape[0] // M_CHUNK):
        rows = pl.ds(ci * M_CHUNK, M_CHUNK)
        _mlstm_chunk(q_ref.at[rows, :], k_ref.at[rows, :], v_ref.at[rows, :], o_ref.at[rows, :],
                     gate_ref.at[rows, :], gb_ref, ng_ref, y_ref.at[rows, :], c_s, m_s)


def _mlstm_chunk(q_ref, k_ref, v_ref, o_ref, gate_ref, gb_ref, ng_ref, y_ref, c_s, m_s):
    L = q_ref.shape[0]
    capped = GATE_CAP * jnp.tanh((gate_ref[...] + gb_ref[...]) / GATE_CAP)
    log_f = jax.nn.log_sigmoid(capped)
    row = lax.broadcasted_iota(jnp.int32, (L, LANES), 0)
    b_all = log_f
    d = 1
    while d < L:
        b_all = b_all + jnp.where(row >= d, pltpu.roll(b_all, d, axis=0), 0.0)
        d *= 2
    i_t = capped.T
    b_t = b_all.T

    tri = lax.broadcasted_iota(jnp.int32, (L, L), 1) <= lax.broadcasted_iota(jnp.int32, (L, L), 0)
    ones = jnp.ones((L, LANES), BF16)
    q_scale = M_QK ** -0.5
    heads = range(M_HEADS)
    qs = [q_ref[:, h * M_QK:(h + 1) * M_QK] for h in heads]
    ks = [k_ref[:, h * M_QK:(h + 1) * M_QK] for h in heads]
    vaugs = [jnp.concatenate([v_ref[:, h * M_V:(h + 1) * M_V], ones], axis=1) for h in heads]
    c_sts = [c_s[h][...] for h in heads]
    m_sts = [m_s[h][0:1, :] for h in heads]
    qks = [lax.dot_general(qs[h], ks[h], (((1,), (1,)), ((), ())), preferred_element_type=F32) for h in heads]
    qcs = [jnp.dot(qs[h], c_sts[h].astype(BF16), preferred_element_type=F32) for h in heads]

    lanes = lambda col: jnp.broadcast_to(col, (L, LANES))
    wide = lambda a, n: jnp.concatenate([a] * n, axis=1)

    gate = []
    for h in heads:
        b_c = lanes(b_all[:, M_HEADS + h:M_HEADS + h + 1])
        i_c = lanes(capped[:, h:h + 1])
        b_row = b_t[M_HEADS + h:M_HEADS + h + 1, :]
        i_row = i_t[h:h + 1, :]
        dmat = jnp.where(tri, wide(b_c, L // LANES) - b_row + i_row, -jnp.inf)
        inter = b_c + m_sts[h]
        m_t = jnp.maximum(inter, lanes(jnp.max(dmat, axis=-1, keepdims=True)))
        w = jnp.exp(dmat - wide(m_t, L // LANES)) * q_scale
        g = jnp.exp(inter - m_t) * q_scale
        b_last = b_c[L - 1:, :]
        w_end = b_last - b_c + i_c
        m_new = jnp.maximum(b_last + m_sts[h], jnp.max(w_end, axis=0, keepdims=True))
        g_end = jnp.exp(b_last + m_sts[h] - m_new)
        gate.append((w, g, m_t, m_new, g_end, jnp.exp(w_end - m_new)))

    n_aug = (M_V + LANES) // LANES
    nums = []
    for h in heads:
        w, g, _, m_new, g_end, w_s = gate[h]
        sc = (qks[h] * w).astype(BF16)
        nums.append(wide(g, n_aug) * qcs[h] + jnp.dot(sc, vaugs[h], preferred_element_type=F32))
        kw = (ks[h].astype(F32) * w_s).astype(BF16)
        c_s[h][...] = wide(g_end, n_aug) * c_sts[h] + lax.dot_general(
            kw, vaugs[h], (((0,), (0,)), ((), ())), preferred_element_type=F32)
        m_s[h][...] = jnp.broadcast_to(m_new, m_s[h].shape)

    for h in heads:
        m_t = gate[h][2]
        den = nums[h][:, M_V:]
        hh = nums[h][:, :M_V] / wide(jnp.maximum(jnp.abs(den), jnp.exp(-m_t)), M_V // LANES)
        msq = lanes(jnp.mean(hh * hh, axis=-1, keepdims=True))
        hn = hh * wide(lax.rsqrt(msq + EPS), M_V // LANES)
        vs = slice(h * M_V, (h + 1) * M_V)
        y_ref[:, vs] = (hn * ng_ref[:, vs] * jax.nn.sigmoid(o_ref[:, vs])).astype(y_ref.dtype)


def _mlstm(qkv, og, gate_bias, norm_g):
    s = qkv.shape[0]
    L = M_CHUNK * M_CHUNKS_PER_STEP
    assert s % L == 0
    qk = M_HEADS * M_QK
    vw = M_HEADS * M_V
    return pl.pallas_call(
        _mlstm_body,
        grid=(s // L,),
        in_specs=[
            pl.BlockSpec((L, qk), lambda i: (i, 0)),
            pl.BlockSpec((L, qk), lambda i: (i, 1)),
            pl.BlockSpec((L, vw), lambda i: (i, 2 * qk // vw)),
            pl.BlockSpec((L, vw), lambda i: (i, 0)),
            pl.BlockSpec((L, LANES), lambda i: (i, vw // LANES)),
            pl.BlockSpec((1, LANES), lambda i: (0, 0)),
            pl.BlockSpec((1, vw), lambda i: (0, 0)),
        ],
        out_specs=pl.BlockSpec((L, vw), lambda i: (i, 0)),
        out_shape=jax.ShapeDtypeStruct((s, vw), BF16),
        scratch_shapes=[pltpu.VMEM((M_QK, M_V + LANES), F32)] * M_HEADS + [pltpu.VMEM((SUBLANES, LANES), F32)] * M_HEADS,
        compiler_params=_params("arbitrary"),
        name="mlstm",
    )(qkv, qkv, qkv, og, og, gate_bias, norm_g)


def _rot_cols(w):
    half = QK_ROPE // 2
    return jnp.concatenate([-w[:, half:], w[:, :half]], axis=1)


def _place(w, start):
    return jnp.pad(w, ((0, 0), (start, HEAD_PAD - start - w.shape[1])))


def _block_diag(w):
    g, bi, bo = w.shape
    eye = jnp.eye(g, dtype=w.dtype)
    return (w[:, :, None, :] * eye[:, None, :, None]).reshape(g * bi, g * bo)


def kernel(x, positions, e_norm_g, e_w_in, e_lru_conv_w, e_lru_conv_b, e_lru_w_a, e_lru_b_a, e_lru_w_x,
           e_lru_b_x, e_lru_lambda, e_q_norm_g, e_w_qb, e_kv_norm_g, e_w_kvb, e_w_out, o_norm_g, o_w_in,
           o_b_igate, o_b_fgate, o_out_norm_g, o_w_out, f_norm_g, f_w_up, f_conv_w, f_conv_b, f_w_down,
           final_norm_g):
    bsz, s, d = x.shape
    assert bsz == 1 and d == D_MODEL
    h = x.reshape(s, d)
    pos = positions.reshape(1, s)
    depth = f_norm_g.shape[0]

    half = QK_ROPE // 2
    inv_freq = ROPE_THETA ** (-jnp.arange(half, dtype=F32) / half)
    freq = inv_freq.reshape(half, 1)
    f_w_up_b = f_w_up.astype(BF16)
    f_w_down_b = f_w_down.astype(BF16)

    for layer in range(depth):
        j = layer // 2
        if layer % 2 == 0:
            w_in = e_w_in[j]
            c_kr = 2 * LRU_WIDTH + Q_LORA + KV_LORA
            w_kr = w_in[:, c_kr:c_kr + QK_ROPE]
            w_kr_placed = jnp.concatenate([_place(w_kr, QK_NOPE), _place(_rot_cols(w_kr), QK_NOPE)], axis=1)
            z, zkr = _norm_matmul(h, e_norm_g[j], [w_in[:, :c_kr].astype(BF16), w_kr_placed.astype(BF16)],
                                  (F32, F32), "even_in")

            y_lru = _rglru(
                z, e_lru_conv_w[j], e_lru_conv_b[j][None, :],
                _block_diag(e_lru_w_a[j]).astype(BF16), e_lru_b_a[j].reshape(1, LRU_WIDTH),
                _block_diag(e_lru_w_x[j]).astype(BF16), e_lru_b_x[j].reshape(1, LRU_WIDTH),
                e_lru_lambda[j][None, :])

            wq = e_w_qb[j].reshape(Q_LORA, MLA_HEADS, QK_NOPE + QK_ROPE)
            wq_a = jnp.pad(wq, ((0, 0), (0, 0), (0, HEAD_PAD - QK_NOPE - QK_ROPE)))
            wq_pe = wq[:, :, QK_NOPE:]
            wq_rot = jnp.concatenate([-wq_pe[:, :, half:], wq_pe[:, :, :half]], axis=2)
            wq_b = jnp.pad(wq_rot, ((0, 0), (0, 0), (QK_NOPE, HEAD_PAD - QK_NOPE - QK_ROPE)))
            wkv = e_w_kvb[j].reshape(KV_LORA, MLA_HEADS, QK_NOPE + V_HEAD)
            wk = jnp.pad(wkv[:, :, :QK_NOPE], ((0, 0), (0, 0), (0, HEAD_PAD - QK_NOPE)))
            hw = MLA_HEADS * HEAD_PAD
            qt, k, vt = _mla_prep(
                z, zkr, pos, e_q_norm_g[j][None, :], e_kv_norm_g[j][None, :],
                wq_a.reshape(Q_LORA, hw).T.astype(BF16), wq_b.reshape(Q_LORA, hw).T.astype(BF16),
                wk.reshape(KV_LORA, hw).astype(BF16),
                wkv[:, :, QK_NOPE:].reshape(KV_LORA, MLA_HEADS * V_HEAD).T.astype(BF16), freq)
            y_mla_t = _attention(qt, k, vt)

            w_out = e_w_out[j].astype(BF16)
            mix = ([y_lru, y_mla_t], [w_out[:LRU_WIDTH], w_out[LRU_WIDTH:]], (False, True))
        else:
            w_in = o_w_in[j]
            w_og = jnp.pad(w_in[:, ODD_QKV_COLS:], ((0, 0), (0, ODD_COLS - w_in.shape[1])))
            qkv, og = _norm_matmul(h, o_norm_g[j], [w_in[:, :ODD_QKV_COLS].astype(BF16), w_og.astype(BF16)],
                                   (BF16, F32), "odd_in")
            gate_bias = jnp.pad(jnp.concatenate([o_b_igate[j], o_b_fgate[j]])[None, :],
                                ((0, 0), (0, LANES - 2 * M_HEADS)))
            y = _mlstm(qkv, og, gate_bias, o_out_norm_g[j][None, :])
            mix = ([y], [o_w_out[j].astype(BF16)], (False,))

        h = _ffn(h, *mix, layer, f_norm_g[:, None, :], f_w_up_b, f_conv_w, f_conv_b[:, None, :], f_w_down_b,
                 final_norm_g[None, :], final_norm=(layer == depth - 1))
    return h.reshape(bsz, s, d)
```

```python
import collections
import functools

import jax
import jax.numpy as jnp
from jax import lax
from jax.experimental import pallas as pl
from jax.experimental.pallas import tpu as pltpu

F32 = jnp.float32
BF16 = jnp.bfloat16

EPS = 1e-6
LANES = 128
SUBLANES = 8
VMEM_LIMIT = 56 * 1024 * 1024

D_MODEL = 1024
LRU_WIDTH = 512
LRU_CONV = 4
LRU_C = 8.0
MLA_HEADS = 8
Q_LORA = 256
KV_LORA = 128
QK_NOPE = 64
QK_ROPE = 32
V_HEAD = 64
V_AUG = 80
ROPE_THETA = 10000.0
HEAD_PAD = 128
M_HEADS = 4
M_QK = 128
M_V = 256
GATE_CAP = 15.0
D_FF = 2816

ODD_QKV_COLS = 2 * M_HEADS * M_QK + M_HEADS * M_V
ODD_COLS = ODD_QKV_COLS + M_HEADS * M_V + LANES

ROW_BLOCK = 512
PROJ_ROW_BLOCK = 1024
ATTN_Q_BLOCK = 1024
ATTN_K_BLOCK = 256
ATTN_ROWS = 32
ATTN_HEADS = 2
ATTN_LANE_SPLIT = 4
ATTN_TRIPS = (4, 2, 1)
ATTN_PV_LAG = 1
LOG2E = 1.4426950408889634
M_CHUNK = 256
M_CHUNKS_PER_STEP = 2
FFN_CHUNK = 256
FFN_ROW_SPLIT = 2


def _params(*sem):
    return pltpu.CompilerParams(dimension_semantics=sem, vmem_limit_bytes=VMEM_LIMIT)


def _rms(x, g):
    inv = lax.rsqrt(jnp.mean(x * x, axis=-1, keepdims=True) + EPS)
    return x * inv * g


def _fold_rows(x, op):
    r = x.shape[0]
    while r > SUBLANES:
        r //= 2
        x = op(x[:r], x[r:])
    return x


def _shift_rows(x, prev_tail, sh):
    t, w = x.shape
    x3 = jnp.concatenate([prev_tail, x], axis=0).reshape(t // SUBLANES + 1, SUBLANES, w)
    rolled = pltpu.roll(x3, sh, axis=1)
    sub = lax.broadcasted_iota(jnp.int32, (t // SUBLANES, SUBLANES, w), 1)
    return jnp.where(sub >= sh, rolled[1:], rolled[:-1]).reshape(t, w)


def _norm_matmul_body(x_ref, g_ref, *refs):
    w_refs, o_refs = refs[:len(refs) // 2], refs[len(refs) // 2:]
    xn = _rms(x_ref[...], g_ref[...]).astype(BF16)
    for w_ref, o_ref in zip(w_refs, o_refs):
        o_ref[...] = jnp.dot(xn, w_ref[...], preferred_element_type=F32).astype(o_ref.dtype)


def _norm_matmul(x, g, ws, out_dtypes, name):
    s, d = x.shape
    tm = min(PROJ_ROW_BLOCK, s)
    return pl.pallas_call(
        _norm_matmul_body,
        grid=(s // tm,),
        in_specs=[pl.BlockSpec((tm, d), lambda i: (i, 0)), pl.BlockSpec((1, d), lambda i: (0, 0))]
        + [pl.BlockSpec(w.shape, lambda i: (0, 0)) for w in ws],
        out_specs=[pl.BlockSpec((tm, w.shape[1]), lambda i: (i, 0)) for w in ws],
        out_shape=[jax.ShapeDtypeStruct((s, w.shape[1]), dt) for w, dt in zip(ws, out_dtypes)],
        compiler_params=_params("parallel"),
        name=name,
    )(x, g.reshape(1, d), *ws)


def _rglru_body(xr_ref, gate_ref, cw_ref, cb_ref, wa_ref, ba_ref, wx_ref, bx_ref, lam_ref,
                o_ref, tail_s, h_s, a_s, b_s):
    @pl.when(pl.program_id(0) == 0)
    def _():
        tail_s[...] = jnp.zeros_like(tail_s)
        h_s[...] = jnp.zeros_like(h_s)

    tm = xr_ref.shape[0]
    xr = xr_ref[...]
    tail = tail_s[...]
    tail_s[...] = xr[tm - SUBLANES:, :]
    xc = cb_ref[...] + _shift_rows(xr, tail, 3) * cw_ref[0:1, :]
    xc = xc + _shift_rows(xr, tail, 2) * cw_ref[1:2, :]
    xc = xc + _shift_rows(xr, tail, 1) * cw_ref[2:3, :]
    xc = xc + xr * cw_ref[3:4, :]

    xcb = xc.astype(BF16)
    r = jax.nn.sigmoid(jnp.dot(xcb, wa_ref[...], preferred_element_type=F32) + ba_ref[...])
    i = jax.nn.sigmoid(jnp.dot(xcb, wx_ref[...], preferred_element_type=F32) + bx_ref[...])
    neg_lam = -lam_ref[...]
    softplus = jnp.maximum(neg_lam, 0.0) + jnp.log1p(jnp.exp(-jnp.abs(neg_lam)))
    log_a = (-LRU_C * r) * softplus
    a = jnp.exp(log_a)
    th = jnp.tanh(log_a)
    y = -2.0 * th
    root = jnp.where(y > 0.0, y * lax.rsqrt(y), 0.0)
    u = root * lax.rsqrt(1.0 - th) * (i * xc)

    w = a.shape[1]
    a = a.reshape(tm // SUBLANES, SUBLANES, w)
    u = u.reshape(tm // SUBLANES, SUBLANES, w)
    sub = lax.broadcasted_iota(jnp.int32, a.shape, 1)
    for d in (1, 2, 4):
        keep = sub >= d
        a_sh = jnp.where(keep, pltpu.roll(a, d, axis=1), 1.0)
        u_sh = jnp.where(keep, pltpu.roll(u, d, axis=1), 0.0)
        u = a * u_sh + u
        a = a * a_sh
    a_s[...] = a.reshape(tm, w)
    b_s[...] = u.reshape(tm, w)

    def group(k, h):
        off = pl.multiple_of(k * SUBLANES, SUBLANES)
        ht = a_s[pl.ds(off, SUBLANES), :] * h + b_s[pl.ds(off, SUBLANES), :]
        b_s[pl.ds(off, SUBLANES), :] = ht
        return ht[SUBLANES - 1:, :]

    h_s[...] = lax.fori_loop(0, tm // SUBLANES, group, h_s[...], unroll=8)
    o_ref[...] = (b_s[...] * jax.nn.gelu(gate_ref[...])).astype(o_ref.dtype)


def _rglru(z, cw, cb, wa, ba, wx, bx, lam):
    s = z.shape[0]
    w = LRU_WIDTH
    tm = min(ROW_BLOCK, s)
    vec = lambda r: pl.BlockSpec((r, w), lambda i: (0, 0))
    return pl.pallas_call(
        _rglru_body,
        grid=(s // tm,),
        in_specs=[
            pl.BlockSpec((tm, w), lambda i: (i, 0)),
            pl.BlockSpec((tm, w), lambda i: (i, 1)),
            vec(LRU_CONV), vec(1), vec(w), vec(1), vec(w), vec(1), vec(1),
        ],
        out_specs=pl.BlockSpec((tm, w), lambda i: (i, 0)),
        out_shape=jax.ShapeDtypeStruct((s, w), BF16),
        scratch_shapes=[
            pltpu.VMEM((SUBLANES, w), F32),
            pltpu.VMEM((1, w), F32),
            pltpu.VMEM((tm, w), F32),
            pltpu.VMEM((tm, w), F32),
        ],
        compiler_params=_params("arbitrary"),
        name="rglru",
    )(z, z, cw, cb, wa, ba, wx, bx, lam)


def _mla_prep_body(zq_ref, zkv_ref, zkr_ref, zkrot_ref, pos_ref, qg_ref, kvg_ref,
                   wqa_ref, wqb_ref, wk_ref, wv_ref, freq_ref, qt_ref, k_ref, vt_ref):
    nt = (((1,), (1,)), ((), ()))
    tm = pos_ref.shape[1]
    ang = freq_ref[...] * pos_ref[...].astype(F32)
    cos_h, sin_h = jnp.cos(ang), jnp.sin(ang)
    zeros = lambda r: jnp.zeros((r, tm), F32)
    cos_t = jnp.concatenate([jnp.ones((QK_NOPE, tm), F32), cos_h, cos_h, zeros(HEAD_PAD - QK_NOPE - QK_ROPE)], axis=0)
    sin_t = jnp.concatenate([zeros(QK_NOPE), sin_h, sin_h, zeros(HEAD_PAD - QK_NOPE - QK_ROPE)], axis=0)
    scale = (QK_NOPE + QK_ROPE) ** -0.5 * LOG2E
    cq_t = cos_t * scale
    sq_t = sin_t * scale
    ck = cos_t.T
    sinv = sin_t.T

    qn = _rms(zq_ref[...], qg_ref[...]).astype(BF16)
    qa_t = lax.dot_general(wqa_ref[...], qn, nt, preferred_element_type=F32)
    qb_t = lax.dot_general(wqb_ref[...], qn, nt, preferred_element_type=F32)
    kvn = _rms(zkv_ref[...], kvg_ref[...]).astype(BF16)
    kn = jnp.dot(kvn, wk_ref[...], preferred_element_type=F32)
    v_t = lax.dot_general(wv_ref[...], kvn, nt, preferred_element_type=F32).astype(BF16)
    tk = vt_ref.shape[2]
    ones = jnp.ones((V_AUG - V_HEAD, tk), BF16)
    for c in range(vt_ref.shape[0]):
        for h in range(MLA_HEADS):
            vt_ref[c, h * V_AUG:h * V_AUG + V_HEAD, :] = v_t[h * V_HEAD:(h + 1) * V_HEAD, c * tk:(c + 1) * tk]
            vt_ref[c, h * V_AUG + V_HEAD:(h + 1) * V_AUG, :] = ones
    kpe = zkr_ref[...] * ck + zkrot_ref[...] * sinv
    for h in range(MLA_HEADS):
        hs = slice(h * HEAD_PAD, (h + 1) * HEAD_PAD)
        qt_ref[0, hs, :] = (qa_t[hs, :] * cq_t + qb_t[hs, :] * sq_t).astype(BF16)
        k_ref[:, hs] = (kn[:, hs] + kpe).astype(BF16)


def _mla_prep(z, zkr, pos, qg, kvg, wqa_t, wqb_t, wk, wv_t, freq):
    s = z.shape[0]
    tm = min(ATTN_Q_BLOCK, s)
    tk = ATTN_K_BLOCK
    assert s % tm == 0 and tm % (2 * tk) == 0
    nb = s // tm
    hw = MLA_HEADS * HEAD_PAD
    vw = MLA_HEADS * V_AUG
    c0 = 2 * LRU_WIDTH
    full = lambda a: pl.BlockSpec(a.shape, lambda i: (0, 0))
    return pl.pallas_call(
        _mla_prep_body,
        grid=(nb,),
        in_specs=[
            pl.BlockSpec((tm, Q_LORA), lambda i: (i, c0 // Q_LORA)),
            pl.BlockSpec((tm, KV_LORA), lambda i: (i, (c0 + Q_LORA) // KV_LORA)),
            pl.BlockSpec((tm, HEAD_PAD), lambda i: (i, 0)),
            pl.BlockSpec((tm, HEAD_PAD), lambda i: (i, 1)),
            pl.BlockSpec((1, tm), lambda i: (0, i)),
            full(qg), full(kvg), full(wqa_t), full(wqb_t), full(wk), full(wv_t), full(freq),
        ],
        out_specs=[
            pl.BlockSpec((1, hw, tm), lambda i: (i, 0, 0)),
            pl.BlockSpec((tm, hw), lambda i: (i, 0)),
            pl.BlockSpec((tm // tk, vw, tk), lambda i: (i, 0, 0)),
        ],
        out_shape=[
            jax.ShapeDtypeStruct((nb, hw, tm), BF16),
            jax.ShapeDtypeStruct((s, hw), BF16),
            jax.ShapeDtypeStruct((s // tk, vw, tk), BF16),
        ],
        compiler_params=_params("parallel"),
        name="mla_prep",
    )(z, z, zkr, zkr, pos, qg, kvg, wqa_t, wqb_t, wk, wv_t, freq)


_AttnHead = collections.namedtuple("_AttnHead", "s_buf smax_buf p_buf acc qt kcols vrows orows")


def _attn_body(qt_ref, k_ref, vt_ref, o_ref, *scratch):
    ns = ATTN_Q_BLOCK // ATTN_K_BLOCK
    ahead = ns - 2
    lag = ATTN_PV_LAG
    npb = lag + 1
    per_head = 2 * ns + npb + 1
    tq = qt_ref.shape[2]
    tk = scratch[0].shape[0]
    i = pl.program_id(1)
    neg = jnp.finfo(F32).min

    def head(hh):
        sc = scratch[hh * per_head:(hh + 1) * per_head]
        return _AttnHead(
            s_buf=sc[:ns], smax_buf=sc[ns:2 * ns], p_buf=sc[2 * ns:2 * ns + npb], acc=sc[-1],
            qt=qt_ref[0, hh * HEAD_PAD:(hh + 1) * HEAD_PAD, :],
            kcols=slice(hh * HEAD_PAD, (hh + 1) * HEAD_PAD), vrows=slice(hh * V_AUG, (hh + 1) * V_AUG),
            orows=slice(hh * V_HEAD, (hh + 1) * V_HEAD))

    heads = [head(hh) for hh in range(ATTN_HEADS)]

    def qk(hd, b, c, lo=0, hi=None):
        off = pl.multiple_of(b * tk, tk)
        s = jnp.dot(k_ref[pl.ds(off, tk), hd.kcols], hd.qt[:, lo:hi], preferred_element_type=F32)
        hd.s_buf[c % ns][:, lo:hi] = s
        hd.smax_buf[c % ns][:, lo:hi] = _fold_rows(s, jnp.maximum)

    def softmax(hd, c, m, lane0=None):
        diag = lane0 is not None
        lane0 = lane0 or 0
        sb = hd.s_buf[c % ns]
        p_buf = hd.p_buf[c % npb]
        chunks = range(0, tk, ATTN_ROWS)
        rest = slice(lane0 + tk, tq)
        join = lambda parts: jnp.concatenate(parts, axis=1) if len(parts) > 1 else parts[0]

        def tri(r):
            s = sb[r:r + ATTN_ROWS, lane0:lane0 + tk]
            key = lax.broadcasted_iota(jnp.int32, s.shape, 0) + r
            return jnp.where(key <= lax.broadcasted_iota(jnp.int32, s.shape, 1), s, neg)

        def load(r):
            if not diag:
                return sb[r:r + ATTN_ROWS, :]
            return join([tri(r)] + ([sb[r:r + ATTN_ROWS, rest]] if rest.start < tq else []))

        if not diag:
            smax = hd.smax_buf[c % ns][...]
        else:
            smax_tri = functools.reduce(jnp.maximum, [_fold_rows(tri(r), jnp.maximum) for r in chunks])
            smax = join([smax_tri] + ([hd.smax_buf[c % ns][:, rest]] if rest.start < tq else []))
        m_old = m[:, lane0:]
        m_new = jnp.maximum(m_old, jnp.max(smax, axis=0, keepdims=True))
        for r in chunks:
            p_buf[r:r + ATTN_ROWS, lane0:] = jnp.exp2(load(r) - m_new).astype(BF16)
        alpha = jnp.exp2(m_old - m_new)
        if lane0:
            p_buf[:, :lane0] = jnp.zeros((tk, lane0), BF16)
            m_new = jnp.concatenate([m[:, :lane0], m_new], axis=1)
            alpha = jnp.concatenate([jnp.ones((1, lane0), F32), alpha], axis=1)
        return m_new, alpha

    def pv(hd, b, c, alpha, lo=0, hi=None):
        hd.acc[:, lo:hi] = alpha[:, lo:hi] * hd.acc[:, lo:hi] + jnp.dot(
            vt_ref[b, hd.vrows, :], hd.p_buf[c % npb][:, lo:hi], preferred_element_type=F32)

    def step(hd, state, b, c, diag):
        m, alphas = state
        b_pv = jnp.maximum(b - lag, 0)
        if not diag:
            part = tq // ATTN_LANE_SPLIT
            for lo in range(0, tq, part):
                qk(hd, b + ahead, c + ahead, lo, lo + part)
                pv(hd, b_pv, c - lag, alphas[0], lo, lo + part)
            m, alpha = softmax(hd, c, m)
            return m, alphas[1:] + (alpha,)
        if c + ahead < ns:
            qk(hd, b + ahead, c + ahead, (c + ahead) * tk)
        pv(hd, b_pv, c - lag, alphas[0])
        m, alpha = softmax(hd, c, m, c * tk if diag else None)
        return m, alphas[1:] + (alpha,)

    def trip(t, carry):
        carry = list(carry)
        for c in range(ns):
            for hh, hd in enumerate(heads):
                carry[hh] = step(hd, carry[hh], t * ns + c, c, diag=False)
        return tuple(carry)

    for hd in heads:
        for c in range(ahead):
            qk(hd, c, c)
        for k in range(1, lag + 1):
            hd.p_buf[-k % npb][...] = jnp.zeros_like(hd.p_buf[-k % npb])
        hd.acc[...] = jnp.zeros_like(hd.acc)
    carry = ((jnp.full((1, tq), -jnp.inf, F32), (jnp.ones((1, tq), F32),) * lag),) * ATTN_HEADS
    first = 0
    for n in ATTN_TRIPS:
        def trips(t, carry, n=n, first=first):
            for k in range(n):
                carry = trip(first + t * n + k, carry)
            return carry

        count = (i - first) // n
        carry = lax.fori_loop(0, count, trips, carry)
        first = first + count * n
    carry = list(carry)

    b0 = i * ns
    for c in range(ns):
        for hh, hd in enumerate(heads):
            carry[hh] = step(hd, carry[hh], b0 + c, c, diag=True)
    for hh, hd in enumerate(heads):
        for k in range(lag):
            pv(hd, b0 + ns - lag + k, ns - lag + k, carry[hh][1][k])
        o_ref[hd.orows, :] = (hd.acc[:V_HEAD, :] / hd.acc[V_HEAD:V_HEAD + 1, :]).astype(o_ref.dtype)


def _attention(qt, k, vt):
    nb, hw, tq = qt.shape
    nkb, vw, tk = vt.shape
    s = k.shape[0]
    hp = ATTN_HEADS
    per_head = ([pltpu.VMEM((tk, tq), F32)] * (tq // tk) + [pltpu.VMEM((SUBLANES, tq), F32)] * (tq // tk)
                + [pltpu.VMEM((tk, tq), BF16)] * (ATTN_PV_LAG + 1) + [pltpu.VMEM((V_AUG, tq), F32)])
    return pl.pallas_call(
        _attn_body,
        grid=(MLA_HEADS // hp, nb),
        in_specs=[
            pl.BlockSpec((1, hp * HEAD_PAD, tq), lambda g, i: (i, g, 0)),
            pl.BlockSpec((s, hp * HEAD_PAD), lambda g, i: (0, g)),
            pl.BlockSpec((nkb, hp * V_AUG, tk), lambda g, i: (0, g, 0)),
        ],
        out_specs=pl.BlockSpec((hp * V_HEAD, tq), lambda g, i: (g, i)),
        out_shape=jax.ShapeDtypeStruct((MLA_HEADS * V_HEAD, s), BF16),
        scratch_shapes=per_head * hp,
        compiler_params=_params("parallel", "parallel"),
        name="attention",
    )(qt, k, vt)


def _ffn_body(*refs, transposed, final_norm):
    n_mix = len(transposed)
    res_ref = refs[0]
    y_refs, w_refs = refs[1:1 + n_mix], refs[1 + n_mix:1 + 2 * n_mix]
    g_ref, wup_ref, cw_ref, cb_ref, wdn_ref, fg_ref, o_ref, tail_s, act_s, xn_s = refs[1 + 2 * n_mix:]

    @pl.when(pl.program_id(0) == 0)
    def _():
        tail_s[...] = jnp.zeros_like(tail_s)

    tm = res_ref.shape[0]
    x = res_ref[...]
    for y_ref, w_ref, tr in zip(y_refs, w_refs, transposed):
        dims = (((0,), (0,)), ((), ())) if tr else (((1,), (0,)), ((), ()))
        x = x + lax.dot_general(y_ref[...], w_ref[...], dims, preferred_element_type=F32)
    xn_s[...] = _rms(x, g_ref[...]).astype(BF16)

    def conv(rows, col, tail):
        cs = pl.ds(col, FFN_CHUNK)
        u = jnp.dot(xn_s[rows, :], wup_ref[:, cs], preferred_element_type=F32)
        y = cb_ref[:, cs] + _shift_rows(u, tail, 2) * cw_ref[0:1, cs]
        y = y + _shift_rows(u, tail, 1) * cw_ref[1:2, cs]
        return y + u * cw_ref[2:3, cs], u[u.shape[0] - SUBLANES:, :]

    sub = tm // FFN_ROW_SPLIT
    for c in range(D_FF // FFN_CHUNK):
        gcol, vcol = c * FFN_CHUNK, D_FF + c * FFN_CHUNK
        g_tail, v_tail = tail_s[:, pl.ds(gcol, FFN_CHUNK)], tail_s[:, pl.ds(vcol, FFN_CHUNK)]
        for r in range(FFN_ROW_SPLIT):
            rows = pl.ds(r * sub, sub)
            gate, g_tail = conv(rows, gcol, g_tail)
            val, v_tail = conv(rows, vcol, v_tail)
            act_s[rows, pl.ds(c * FFN_CHUNK, FFN_CHUNK)] = (jax.nn.silu(gate) * val).astype(BF16)
        tail_s[:, pl.ds(gcol, FFN_CHUNK)] = g_tail
        tail_s[:, pl.ds(vcol, FFN_CHUNK)] = v_tail

    out = x + jnp.dot(act_s[...], wdn_ref[...], preferred_element_type=F32)
    if final_norm:
        out = _rms(out, fg_ref[...])
    o_ref[...] = out


def _ffn(res, ys, ws, transposed, layer, g, wup, cw, cb, wdn, fg, final_norm):
    s, d = res.shape
    tm = min(ROW_BLOCK, s)
    const = lambda a: pl.BlockSpec(a.shape, lambda i: (0, 0), pipeline_mode=pl.Buffered(1))
    stacked = lambda a: pl.BlockSpec((None,) + a.shape[1:], lambda i: (layer, 0, 0), pipeline_mode=pl.Buffered(1))
    y_specs = [pl.BlockSpec((y.shape[0], tm), lambda i: (0, i)) if tr else
               pl.BlockSpec((tm, y.shape[1]), lambda i: (i, 0)) for y, tr in zip(ys, transposed)]
    return pl.pallas_call(
        functools.partial(_ffn_body, transposed=transposed, final_norm=final_norm),
        grid=(s // tm,),
        in_specs=[pl.BlockSpec((tm, d), lambda i: (i, 0))] + y_specs + [const(w) for w in ws]
        + [stacked(g), stacked(wup), stacked(cw), stacked(cb), stacked(wdn), const(fg)],
        out_specs=pl.BlockSpec((tm, d), lambda i: (i, 0)),
        out_shape=jax.ShapeDtypeStruct((s, d), F32),
        scratch_shapes=[pltpu.VMEM((SUBLANES, 2 * D_FF), F32), pltpu.VMEM((tm, D_FF), BF16),
                        pltpu.VMEM((tm, d), BF16)],
        compiler_params=_params("arbitrary"),
        name="ffn_final" if final_norm else "ffn",
    )(res, *ys, *ws, g, wup, cw, cb, wdn, fg)


def _mlstm_body(q_ref, k_ref, v_ref, o_ref, gate_ref, gb_ref, ng_ref, y_ref, *state):
    c_s, m_s = state[:M_HEADS], state[M_HEADS:]

    @pl.when(pl.program_id(0) == 0)
    def _():
        for ref in state:
            ref[...] = jnp.zeros_like(ref)

    for ci in range(q_ref.shape[0] // M_CHUNK):
        rows = pl.ds(ci * M_CHUNK, M_CHUNK)
        _mlstm_chunk(q_ref.at[rows, :], k_ref.at[rows, :], v_ref.at[rows, :], o_ref.at[rows, :],
                     gate_ref.at[rows, :], gb_ref, ng_ref, y_ref.at[rows, :], c_s, m_s)


def _mlstm_chunk(q_ref, k_ref, v_ref, o_ref, gate_ref, gb_ref, ng_ref, y_ref, c_s, m_s):
    L = q_ref.shape[0]
    capped = GATE_CAP * jnp.tanh((gate_ref[...] + gb_ref[...]) / GATE_CAP)
    log_f = jax.nn.log_sigmoid(capped)
    row = lax.broadcasted_iota(jnp.int32, (L, LANES), 0)
    b_all = log_f
    d = 1
    while d < L:
        b_all = b_all + jnp.where(row >= d, pltpu.roll(b_all, d, axis=0), 0.0)
        d *= 2
    i_t = capped.T
    b_t = b_all.T

    tri = lax.broadcasted_iota(jnp.int32, (L, L), 1) <= lax.broadcasted_iota(jnp.int32, (L, L), 0)
    ones = jnp.ones((L, LANES), BF16)
    q_scale = M_QK ** -0.5
    heads = range(M_HEADS)
    qs = [q_ref[:, h * M_QK:(h + 1) * M_QK] for h in heads]
    ks = [k_ref[:, h * M_QK:(h + 1) * M_QK] for h in heads]
    vaugs = [jnp.concatenate([v_ref[:, h * M_V:(h + 1) * M_V], ones], axis=1) for h in heads]
    c_sts = [c_s[h][...] for h in heads]
    m_sts = [m_s[h][0:1, :] for h in heads]
    qks = [lax.dot_general(qs[h], ks[h], (((1,), (1,)), ((), ())), preferred_element_type=F32) for h in heads]
    qcs = [jnp.dot(qs[h], c_sts[h].astype(BF16), preferred_element_type=F32) for h in heads]

    lanes = lambda col: jnp.broadcast_to(col, (L, LANES))
    wide = lambda a, n: jnp.concatenate([a] * n, axis=1)

    gate = []
    for h in heads:
        b_c = lanes(b_all[:, M_HEADS + h:M_HEADS + h + 1])
        i_c = lanes(capped[:, h:h + 1])
        b_row = b_t[M_HEADS + h:M_HEADS + h + 1, :]
        i_row = i_t[h:h + 1, :]
        dmat = jnp.where(tri, wide(b_c, L // LANES) - b_row + i_row, -jnp.inf)
        inter = b_c + m_sts[h]
        m_t = jnp.maximum(inter, lanes(jnp.max(dmat, axis=-1, keepdims=True)))
        w = jnp.exp(dmat - wide(m_t, L // LANES)) * q_scale
        g = jnp.exp(inter - m_t) * q_scale
        b_last = b_c[L - 1:, :]
        w_end = b_last - b_c + i_c
        m_new = jnp.maximum(b_last + m_sts[h], jnp.max(w_end, axis=0, keepdims=True))
        g_end = jnp.exp(b_last + m_sts[h] - m_new)
        gate.append((w, g, m_t, m_new, g_end, jnp.exp(w_end - m_new)))

    n_aug = (M_V + LANES) // LANES
    nums = []
    for h in heads:
        w, g, _, m_new, g_end, w_s = gate[h]
        sc = (qks[h] * w).astype(BF16)
        nums.append(wide(g, n_aug) * qcs[h] + jnp.dot(sc, vaugs[h], preferred_element_type=F32))
        kw = (ks[h].astype(F32) * w_s).astype(BF16)
        c_s[h][...] = wide(g_end, n_aug) * c_sts[h] + lax.dot_general(
            kw, vaugs[h], (((0,), (0,)), ((), ())), preferred_element_type=F32)
        m_s[h][...] = jnp.broadcast_to(m_new, m_s[h].shape)

    for h in heads:
        m_t = gate[h][2]
        den = nums[h][:, M_V:]
        hh = nums[h][:, :M_V] / wide(jnp.maximum(jnp.abs(den), jnp.exp(-m_t)), M_V // LANES)
        msq = lanes(jnp.mean(hh * hh, axis=-1, keepdims=True))
        hn = hh * wide(lax.rsqrt(msq + EPS), M_V // LANES)
        vs = slice(h * M_V, (h + 1) * M_V)
        y_ref[:, vs] = (hn * ng_ref[:, vs] * jax.nn.sigmoid(o_ref[:, vs])).astype(y_ref.dtype)


def _mlstm(qkv, og, gate_bias, norm_g):
    s = qkv.shape[0]
    L = M_CHUNK * M_CHUNKS_PER_STEP
    assert s % L == 0
    qk = M_HEADS * M_QK
    vw = M_HEADS * M_V
    return pl.pallas_call(
        _mlstm_body,
        grid=(s // L,),
        in_specs=[
            pl.BlockSpec((L, qk), lambda i: (i, 0)),
            pl.BlockSpec((L, qk), lambda i: (i, 1)),
            pl.BlockSpec((L, vw), lambda i: (i, 2 * qk // vw)),
            pl.BlockSpec((L, vw), lambda i: (i, 0)),
            pl.BlockSpec((L, LANES), lambda i: (i, vw // LANES)),
            pl.BlockSpec((1, LANES), lambda i: (0, 0)),
            pl.BlockSpec((1, vw), lambda i: (0, 0)),
        ],
        out_specs=pl.BlockSpec((L, vw), lambda i: (i, 0)),
        out_shape=jax.ShapeDtypeStruct((s, vw), BF16),
        scratch_shapes=[pltpu.VMEM((M_QK, M_V + LANES), F32)] * M_HEADS + [pltpu.VMEM((SUBLANES, LANES), F32)] * M_HEADS,
        compiler_params=_params("arbitrary"),
        name="mlstm",
    )(qkv, qkv, qkv, og, og, gate_bias, norm_g)


def _rot_cols(w):
    half = QK_ROPE // 2
    return jnp.concatenate([-w[:, half:], w[:, :half]], axis=1)


def _place(w, start):
    return jnp.pad(w, ((0, 0), (start, HEAD_PAD - start - w.shape[1])))


def _block_diag(w):
    g, bi, bo = w.shape
    eye = jnp.eye(g, dtype=w.dtype)
    return (w[:, :, None, :] * eye[:, None, :, None]).reshape(g * bi, g * bo)


def kernel(x, positions, e_norm_g, e_w_in, e_lru_conv_w, e_lru_conv_b, e_lru_w_a, e_lru_b_a, e_lru_w_x,
           e_lru_b_x, e_lru_lambda, e_q_norm_g, e_w_qb, e_kv_norm_g, e_w_kvb, e_w_out, o_norm_g, o_w_in,
           o_b_igate, o_b_fgate, o_out_norm_g, o_w_out, f_norm_g, f_w_up, f_conv_w, f_conv_b, f_w_down,
           final_norm_g):
    bsz, s, d = x.shape
    assert bsz == 1 and d == D_MODEL
    h = x.reshape(s, d)
    pos = positions.reshape(1, s)
    depth = f_norm_g.shape[0]

    half = QK_ROPE // 2
    inv_freq = ROPE_THETA ** (-jnp.arange(half, dtype=F32) / half)
    freq = inv_freq.reshape(half, 1)
    f_w_up_b = f_w_up.astype(BF16)
    f_w_down_b = f_w_down.astype(BF16)

    for layer in range(depth):
        j = layer // 2
        if layer % 2 == 0:
            w_in = e_w_in[j]
            c_kr = 2 * LRU_WIDTH + Q_LORA + KV_LORA
            w_kr = w_in[:, c_kr:c_kr + QK_ROPE]
            w_kr_placed = jnp.concatenate([_place(w_kr, QK_NOPE), _place(_rot_cols(w_kr), QK_NOPE)], axis=1)
            z, zkr = _norm_matmul(h, e_norm_g[j], [w_in[:, :c_kr].astype(BF16), w_kr_placed.astype(BF16)],
                                  (F32, F32), "even_in")

            y_lru = _rglru(
                z, e_lru_conv_w[j], e_lru_conv_b[j][None, :],
                _block_diag(e_lru_w_a[j]).astype(BF16), e_lru_b_a[j].reshape(1, LRU_WIDTH),
                _block_diag(e_lru_w_x[j]).astype(BF16), e_lru_b_x[j].reshape(1, LRU_WIDTH),
                e_lru_lambda[j][None, :])

            wq = e_w_qb[j].reshape(Q_LORA, MLA_HEADS, QK_NOPE + QK_ROPE)
            wq_a = jnp.pad(wq, ((0, 0), (0, 0), (0, HEAD_PAD - QK_NOPE - QK_ROPE)))
            wq_pe = wq[:, :, QK_NOPE:]
            wq_rot = jnp.concatenate([-wq_pe[:, :, half:], wq_pe[:, :, :half]], axis=2)
            wq_b = jnp.pad(wq_rot, ((0, 0), (0, 0), (QK_NOPE, HEAD_PAD - QK_NOPE - QK_ROPE)))
            wkv = e_w_kvb[j].reshape(KV_LORA, MLA_HEADS, QK_NOPE + V_HEAD)
            wk = jnp.pad(wkv[:, :, :QK_NOPE], ((0, 0), (0, 0), (0, HEAD_PAD - QK_NOPE)))
            hw = MLA_HEADS * HEAD_PAD
            qt, k, vt = _mla_prep(
                z, zkr, pos, e_q_norm_g[j][None, :], e_kv_norm_g[j][None, :],
                wq_a.reshape(Q_LORA, hw).T.astype(BF16), wq_b.reshape(Q_LORA, hw).T.astype(BF16),
                wk.reshape(KV_LORA, hw).astype(BF16),
                wkv[:, :, QK_NOPE:].reshape(KV_LORA, MLA_HEADS * V_HEAD).T.astype(BF16), freq)
            y_mla_t = _attention(qt, k, vt)

            w_out = e_w_out[j].astype(BF16)
            mix = ([y_lru, y_mla_t], [w_out[:LRU_WIDTH], w_out[LRU_WIDTH:]], (False, True))
        else:
            w_in = o_w_in[j]
            w_og = jnp.pad(w_in[:, ODD_QKV_COLS:], ((0, 0), (0, ODD_COLS - w_in.shape[1])))
            qkv, og = _norm_matmul(h, o_norm_g[j], [w_in[:, :ODD_QKV_COLS].astype(BF16), w_og.astype(BF16)],
                                   (BF16, F32), "odd_in")
            gate_bias = jnp.pad(jnp.concatenate([o_b_igate[j], o_b_fgate[j]])[None, :],
                                ((0, 0), (0, LANES - 2 * M_HEADS)))
            y = _mlstm(qkv, og, gate_bias, o_out_norm_g[j][None, :])
            mix = ([y], [o_w_out[j].astype(BF16)], (False,))

        h = _ffn(h, *mix, layer, f_norm_g[:, None, :], f_w_up_b, f_conv_w, f_conv_b[:, None, :], f_w_down_b,
                 final_norm_g[None, :], final_norm=(layer == depth - 1))
    return h.reshape(bsz, s, d)
```

```python
import collections
import functools

import jax
import jax.numpy as jnp
from jax import lax
from jax.experimental import pallas as pl
from jax.experimental.pallas import tpu as pltpu

F32 = jnp.float32
BF16 = jnp.bfloat16

EPS = 1e-6
LANES = 128
SUBLANES = 8
VMEM_LIMIT = 56 * 1024 * 1024

D_MODEL = 1024
LRU_WIDTH = 512
LRU_CONV = 4
LRU_C = 8.0
MLA_HEADS = 8
Q_LORA = 256
KV_LORA = 128
QK_NOPE = 64
QK_ROPE = 32
V_HEAD = 64
V_AUG = 80
ROPE_THETA = 10000.0
HEAD_PAD = 128
M_HEADS = 4
M_QK = 128
M_V = 256
GATE_CAP = 15.0
D_FF = 2816

ODD_QKV_COLS = 2 * M_HEADS * M_QK + M_HEADS * M_V
ODD_COLS = ODD_QKV_COLS + M_HEADS * M_V + LANES

ROW_BLOCK = 512
PROJ_ROW_BLOCK = 1024
ATTN_Q_BLOCK = 1024
ATTN_K_BLOCK = 256
ATTN_ROWS = 32
ATTN_HEADS = 2
ATTN_LANE_SPLIT = 4
ATTN_TRIPS = (4, 2, 1)
ATTN_PV_LAG = 1
LOG2E = 1.4426950408889634
M_CHUNK = 256
M_CHUNKS_PER_STEP = 2
FFN_CHUNK = 256
FFN_ROW_SPLIT = 2


def _params(*sem):
    return pltpu.CompilerParams(dimension_semantics=sem, vmem_limit_bytes=VMEM_LIMIT)


def _rms(x, g):
    inv = lax.rsqrt(jnp.mean(x * x, axis=-1, keepdims=True) + EPS)
    return x * inv * g


def _fold_rows(x, op):
    r = x.shape[0]
    while r > SUBLANES:
        r //= 2
        x = op(x[:r], x[r:])
    return x


def _shift_rows(x, prev_tail, sh):
    t, w = x.shape
    x3 = jnp.concatenate([prev_tail, x], axis=0).reshape(t // SUBLANES + 1, SUBLANES, w)
    rolled = pltpu.roll(x3, sh, axis=1)
    sub = lax.broadcasted_iota(jnp.int32, (t // SUBLANES, SUBLANES, w), 1)
    return jnp.where(sub >= sh, rolled[1:], rolled[:-1]).reshape(t, w)


def _norm_matmul_body(x_ref, g_ref, *refs):
    w_refs, o_refs = refs[:len(refs) // 2], refs[len(refs) // 2:]
    xn = _rms(x_ref[...], g_ref[...]).astype(BF16)
    for w_ref, o_ref in zip(w_refs, o_refs):
        o_ref[...] = jnp.dot(xn, w_ref[...], preferred_element_type=F32).astype(o_ref.dtype)


def _norm_matmul(x, g, ws, out_dtypes, name):
    s, d = x.shape
    tm = min(PROJ_ROW_BLOCK, s)
    return pl.pallas_call(
        _norm_matmul_body,
        grid=(s // tm,),
        in_specs=[pl.BlockSpec((tm, d), lambda i: (i, 0)), pl.BlockSpec((1, d), lambda i: (0, 0))]
        + [pl.BlockSpec(w.shape, lambda i: (0, 0)) for w in ws],
        out_specs=[pl.BlockSpec((tm, w.shape[1]), lambda i: (i, 0)) for w in ws],
        out_shape=[jax.ShapeDtypeStruct((s, w.shape[1]), dt) for w, dt in zip(ws, out_dtypes)],
        compiler_params=_params("parallel"),
        name=name,
    )(x, g.reshape(1, d), *ws)


def _rglru_body(xr_ref, gate_ref, cw_ref, cb_ref, wa_ref, ba_ref, wx_ref, bx_ref, lam_ref,
                o_ref, tail_s, h_s):
    @pl.when(pl.program_id(0) == 0)
    def _():
        tail_s[...] = jnp.zeros_like(tail_s)
        h_s[...] = jnp.zeros_like(h_s)

    tm = xr_ref.shape[0]
    xr = xr_ref[...]
    tail = tail_s[...]
    tail_s[...] = xr[tm - SUBLANES:, :]
    xc = cb_ref[...] + _shift_rows(xr, tail, 3) * cw_ref[0:1, :]
    xc = xc + _shift_rows(xr, tail, 2) * cw_ref[1:2, :]
    xc = xc + _shift_rows(xr, tail, 1) * cw_ref[2:3, :]
    xc = xc + xr * cw_ref[3:4, :]

    xcb = xc.astype(BF16)
    r = jax.nn.sigmoid(jnp.dot(xcb, wa_ref[...], preferred_element_type=F32) + ba_ref[...])
    i = jax.nn.sigmoid(jnp.dot(xcb, wx_ref[...], preferred_element_type=F32) + bx_ref[...])
    neg_lam = -lam_ref[...]
    softplus = jnp.maximum(neg_lam, 0.0) + jnp.log1p(jnp.exp(-jnp.abs(neg_lam)))
    log_a = (-LRU_C * r) * softplus
    a = jnp.exp(log_a)
    th = jnp.tanh(log_a)
    y = -2.0 * th
    root = jnp.where(y > 0.0, y * lax.rsqrt(y), 0.0)
    u = root * lax.rsqrt(1.0 - th) * (i * xc)

    w = a.shape[1]
    a = a.reshape(tm // SUBLANES, SUBLANES, w)
    u = u.reshape(tm // SUBLANES, SUBLANES, w)
    sub = lax.broadcasted_iota(jnp.int32, a.shape, 1)
    for d in (1, 2, 4):
        keep = sub >= d
        a_sh = jnp.where(keep, pltpu.roll(a, d, axis=1), 1.0)
        u_sh = jnp.where(keep, pltpu.roll(u, d, axis=1), 0.0)
        u = a * u_sh + u
        a = a * a_sh
    gelu = jax.nn.gelu(gate_ref[...])
    h = h_s[...]
    pack = 2 * SUBLANES
    for r in range(0, tm, pack):
        hts = []
        for k in range(r // SUBLANES, (r + pack) // SUBLANES):
            hts.append(a[k] * h + u[k])
            h = hts[-1][SUBLANES - 1:, :]
        o_ref[r:r + pack, :] = (jnp.concatenate(hts, axis=0) * gelu[r:r + pack, :]).astype(o_ref.dtype)
    h_s[...] = h


def _rglru(z, cw, cb, wa, ba, wx, bx, lam):
    s = z.shape[0]
    w = LRU_WIDTH
    tm = min(ROW_BLOCK, s)
    vec = lambda r: pl.BlockSpec((r, w), lambda i: (0, 0))
    return pl.pallas_call(
        _rglru_body,
        grid=(s // tm,),
        in_specs=[
            pl.BlockSpec((tm, w), lambda i: (i, 0)),
            pl.BlockSpec((tm, w), lambda i: (i, 1)),
            vec(LRU_CONV), vec(1), vec(w), vec(1), vec(w), vec(1), vec(1),
        ],
        out_specs=pl.BlockSpec((tm, w), lambda i: (i, 0)),
        out_shape=jax.ShapeDtypeStruct((s, w), BF16),
        scratch_shapes=[
            pltpu.VMEM((SUBLANES, w), F32),
            pltpu.VMEM((1, w), F32),
        ],
        compiler_params=_params("arbitrary"),
        name="rglru",
    )(z, z, cw, cb, wa, ba, wx, bx, lam)


def _mla_prep_body(zq_ref, zkv_ref, zkr_ref, zkrot_ref, pos_ref, qg_ref, kvg_ref,
                   wqa_ref, wqb_ref, wk_ref, wv_ref, freq_ref, qt_ref, k_ref, vt_ref):
    nt = (((1,), (1,)), ((), ()))
    tm = pos_ref.shape[1]
    ang = freq_ref[...] * pos_ref[...].astype(F32)
    cos_h, sin_h = jnp.cos(ang), jnp.sin(ang)
    zeros = lambda r: jnp.zeros((r, tm), F32)
    cos_t = jnp.concatenate([jnp.ones((QK_NOPE, tm), F32), cos_h, cos_h, zeros(HEAD_PAD - QK_NOPE - QK_ROPE)], axis=0)
    sin_t = jnp.concatenate([zeros(QK_NOPE), sin_h, sin_h, zeros(HEAD_PAD - QK_NOPE - QK_ROPE)], axis=0)
    scale = (QK_NOPE + QK_ROPE) ** -0.5 * LOG2E
    cq_t = cos_t * scale
    sq_t = sin_t * scale
    ck = cos_t.T
    sinv = sin_t.T

    qn = _rms(zq_ref[...], qg_ref[...]).astype(BF16)
    qa_t = lax.dot_general(wqa_ref[...], qn, nt, preferred_element_type=F32)
    qb_t = lax.dot_general(wqb_ref[...], qn, nt, preferred_element_type=F32)
    kvn = _rms(zkv_ref[...], kvg_ref[...]).astype(BF16)
    kn = jnp.dot(kvn, wk_ref[...], preferred_element_type=F32)
    v_t = lax.dot_general(wv_ref[...], kvn, nt, preferred_element_type=F32).astype(BF16)
    tk = vt_ref.shape[2]
    ones = jnp.ones((V_AUG - V_HEAD, tk), BF16)
    for c in range(vt_ref.shape[0]):
        for h in range(MLA_HEADS):
            vt_ref[c, h * V_AUG:h * V_AUG + V_HEAD, :] = v_t[h * V_HEAD:(h + 1) * V_HEAD, c * tk:(c + 1) * tk]
            vt_ref[c, h * V_AUG + V_HEAD:(h + 1) * V_AUG, :] = ones
    kpe = zkr_ref[...] * ck + zkrot_ref[...] * sinv
    for h in range(MLA_HEADS):
        hs = slice(h * HEAD_PAD, (h + 1) * HEAD_PAD)
        qt_ref[0, hs, :] = (qa_t[hs, :] * cq_t + qb_t[hs, :] * sq_t).astype(BF16)
        k_ref[:, hs] = (kn[:, hs] + kpe).astype(BF16)


def _mla_prep(z, zkr, pos, qg, kvg, wqa_t, wqb_t, wk, wv_t, freq):
    s = z.shape[0]
    tm = min(ATTN_Q_BLOCK, s)
    tk = ATTN_K_BLOCK
    assert s % tm == 0 and tm % (2 * tk) == 0
    nb = s // tm
    hw = MLA_HEADS * HEAD_PAD
    vw = MLA_HEADS * V_AUG
    c0 = 2 * LRU_WIDTH
    full = lambda a: pl.BlockSpec(a.shape, lambda i: (0, 0))
    return pl.pallas_call(
        _mla_prep_body,
        grid=(nb,),
        in_specs=[
            pl.BlockSpec((tm, Q_LORA), lambda i: (i, c0 // Q_LORA)),
            pl.BlockSpec((tm, KV_LORA), lambda i: (i, (c0 + Q_LORA) // KV_LORA)),
            pl.BlockSpec((tm, HEAD_PAD), lambda i: (i, 0)),
            pl.BlockSpec((tm, HEAD_PAD), lambda i: (i, 1)),
            pl.BlockSpec((1, tm), lambda i: (0, i)),
            full(qg), full(kvg), full(wqa_t), full(wqb_t), full(wk), full(wv_t), full(freq),
        ],
        out_specs=[
            pl.BlockSpec((1, hw, tm), lambda i: (i, 0, 0)),
            pl.BlockSpec((tm, hw), lambda i: (i, 0)),
            pl.BlockSpec((tm // tk, vw, tk), lambda i: (i, 0, 0)),
        ],
        out_shape=[
            jax.ShapeDtypeStruct((nb, hw, tm), BF16),
            jax.ShapeDtypeStruct((s, hw), BF16),
            jax.ShapeDtypeStruct((s // tk, vw, tk), BF16),
        ],
        compiler_params=_params("parallel"),
        name="mla_prep",
    )(z, z, zkr, zkr, pos, qg, kvg, wqa_t, wqb_t, wk, wv_t, freq)


_AttnHead = collections.namedtuple("_AttnHead", "s_buf smax_buf p_buf acc qt kcols vrows orows")


def _attn_body(qt_ref, k_ref, vt_ref, o_ref, *scratch):
    ns = ATTN_Q_BLOCK // ATTN_K_BLOCK
    ahead = ns - 2
    lag = ATTN_PV_LAG
    npb = lag + 1
    per_head = 2 * ns + npb + 1
    tq = qt_ref.shape[2]
    tk = scratch[0].shape[0]
    i = pl.program_id(1)
    neg = jnp.finfo(F32).min

    def head(hh):
        sc = scratch[hh * per_head:(hh + 1) * per_head]
        return _AttnHead(
            s_buf=sc[:ns], smax_buf=sc[ns:2 * ns], p_buf=sc[2 * ns:2 * ns + npb], acc=sc[-1],
            qt=qt_ref[0, hh * HEAD_PAD:(hh + 1) * HEAD_PAD, :],
            kcols=slice(hh * HEAD_PAD, (hh + 1) * HEAD_PAD), vrows=slice(hh * V_AUG, (hh + 1) * V_AUG),
            orows=slice(hh * V_HEAD, (hh + 1) * V_HEAD))

    heads = [head(hh) for hh in range(ATTN_HEADS)]

    def qk(hd, b, c, lo=0, hi=None):
        off = pl.multiple_of(b * tk, tk)
        s = jnp.dot(k_ref[pl.ds(off, tk), hd.kcols], hd.qt[:, lo:hi], preferred_element_type=F32)
        hd.s_buf[c % ns][:, lo:hi] = s
        hd.smax_buf[c % ns][:, lo:hi] = _fold_rows(s, jnp.maximum)

    def softmax(hd, c, m, lane0=None):
        diag = lane0 is not None
        lane0 = lane0 or 0
        sb = hd.s_buf[c % ns]
        p_buf = hd.p_buf[c % npb]
        chunks = range(0, tk, ATTN_ROWS)
        rest = slice(lane0 + tk, tq)
        join = lambda parts: jnp.concatenate(parts, axis=1) if len(parts) > 1 else parts[0]

        def tri(r):
            s = sb[r:r + ATTN_ROWS, lane0:lane0 + tk]
            key = lax.broadcasted_iota(jnp.int32, s.shape, 0) + r
            return jnp.where(key <= lax.broadcasted_iota(jnp.int32, s.shape, 1), s, neg)

        def load(r):
            if not diag:
                return sb[r:r + ATTN_ROWS, :]
            return join([tri(r)] + ([sb[r:r + ATTN_ROWS, rest]] if rest.start < tq else []))

        if not diag:
            smax = hd.smax_buf[c % ns][...]
        else:
            smax_tri = functools.reduce(jnp.maximum, [_fold_rows(tri(r), jnp.maximum) for r in chunks])
            smax = join([smax_tri] + ([hd.smax_buf[c % ns][:, rest]] if rest.start < tq else []))
        m_old = m[:, lane0:]
        m_new = jnp.maximum(m_old, jnp.max(smax, axis=0, keepdims=True))
        for r in chunks:
            p_buf[r:r + ATTN_ROWS, lane0:] = jnp.exp2(load(r) - m_new).astype(BF16)
        alpha = jnp.exp2(m_old - m_new)
        if lane0:
            p_buf[:, :lane0] = jnp.zeros((tk, lane0), BF16)
            m_new = jnp.concatenate([m[:, :lane0], m_new], axis=1)
            alpha = jnp.concatenate([jnp.ones((1, lane0), F32), alpha], axis=1)
        return m_new, alpha

    def pv(hd, b, c, alpha, lo=0, hi=None):
        hd.acc[:, lo:hi] = alpha[:, lo:hi] * hd.acc[:, lo:hi] + jnp.dot(
            vt_ref[b, hd.vrows, :], hd.p_buf[c % npb][:, lo:hi], preferred_element_type=F32)

    def step(hd, state, b, c, diag):
        m, alphas = state
        b_pv = jnp.maximum(b - lag, 0)
        if not diag:
            part = tq // ATTN_LANE_SPLIT
            for lo in range(0, tq, part):
                qk(hd, b + ahead, c + ahead, lo, lo + part)
                pv(hd, b_pv, c - lag, alphas[0], lo, lo + part)
            m, alpha = softmax(hd, c, m)
            return m, alphas[1:] + (alpha,)
        if c + ahead < ns:
            qk(hd, b + ahead, c + ahead, (c + ahead) * tk)
        pv(hd, b_pv, c - lag, alphas[0])
        m, alpha = softmax(hd, c, m, c * tk if diag else None)
        return m, alphas[1:] + (alpha,)

    def trip(t, carry):
        carry = list(carry)
        for c in range(ns):
            for hh, hd in enumerate(heads):
                carry[hh] = step(hd, carry[hh], t * ns + c, c, diag=False)
        return tuple(carry)

    for hd in heads:
        for c in range(ahead):
            qk(hd, c, c)
        for k in range(1, lag + 1):
            hd.p_buf[-k % npb][...] = jnp.zeros_like(hd.p_buf[-k % npb])
        hd.acc[...] = jnp.zeros_like(hd.acc)
    carry = ((jnp.full((1, tq), -jnp.inf, F32), (jnp.ones((1, tq), F32),) * lag),) * ATTN_HEADS
    first = 0
    for n in ATTN_TRIPS:
        def trips(t, carry, n=n, first=first):
            for k in range(n):
                carry = trip(first + t * n + k, carry)
            return carry

        count = (i - first) // n
        carry = lax.fori_loop(0, count, trips, carry)
        first = first + count * n
    carry = list(carry)

    b0 = i * ns
    for c in range(ns):
        for hh, hd in enumerate(heads):
            carry[hh] = step(hd, carry[hh], b0 + c, c, diag=True)
    for hh, hd in enumerate(heads):
        for k in range(lag):
            pv(hd, b0 + ns - lag + k, ns - lag + k, carry[hh][1][k])
        o_ref[hd.orows, :] = (hd.acc[:V_HEAD, :] / hd.acc[V_HEAD:V_HEAD + 1, :]).astype(o_ref.dtype)


def _attention(qt, k, vt):
    nb, hw, tq = qt.shape
    nkb, vw, tk = vt.shape
    s = k.shape[0]
    hp = ATTN_HEADS
    per_head = ([pltpu.VMEM((tk, tq), F32)] * (tq // tk) + [pltpu.VMEM((SUBLANES, tq), F32)] * (tq // tk)
                + [pltpu.VMEM((tk, tq), BF16)] * (ATTN_PV_LAG + 1) + [pltpu.VMEM((V_AUG, tq), F32)])
    return pl.pallas_call(
        _attn_body,
        grid=(MLA_HEADS // hp, nb),
        in_specs=[
            pl.BlockSpec((1, hp * HEAD_PAD, tq), lambda g, i: (i, g, 0)),
            pl.BlockSpec((s, hp * HEAD_PAD), lambda g, i: (0, g)),
            pl.BlockSpec((nkb, hp * V_AUG, tk), lambda g, i: (0, g, 0)),
        ],
        out_specs=pl.BlockSpec((hp * V_HEAD, tq), lambda g, i: (g, i)),
        out_shape=jax.ShapeDtypeStruct((MLA_HEADS * V_HEAD, s), BF16),
        scratch_shapes=per_head * hp,
        compiler_params=_params("parallel", "parallel"),
        name="attention",
    )(qt, k, vt)


def _ffn_body(*refs, transposed, final_norm):
    n_mix = len(transposed)
    res_ref = refs[0]
    y_refs, w_refs = refs[1:1 + n_mix], refs[1 + n_mix:1 + 2 * n_mix]
    g_ref, wup_ref, cw_ref, cb_ref, wdn_ref, fg_ref, o_ref, tail_s, act_s, xn_s = refs[1 + 2 * n_mix:]

    @pl.when(pl.program_id(0) == 0)
    def _():
        tail_s[...] = jnp.zeros_like(tail_s)

    tm = res_ref.shape[0]
    x = res_ref[...]
    for y_ref, w_ref, tr in zip(y_refs, w_refs, transposed):
        dims = (((0,), (0,)), ((), ())) if tr else (((1,), (0,)), ((), ()))
        x = x + lax.dot_general(y_ref[...], w_ref[...], dims, preferred_element_type=F32)
    xn_s[...] = _rms(x, g_ref[...]).astype(BF16)

    def conv(rows, col, tail):
        cs = pl.ds(col, FFN_CHUNK)
        u = jnp.dot(xn_s[rows, :], wup_ref[:, cs], preferred_element_type=F32)
        y = cb_ref[:, cs] + _shift_rows(u, tail, 2) * cw_ref[0:1, cs]
        y = y + _shift_rows(u, tail, 1) * cw_ref[1:2, cs]
        return y + u * cw_ref[2:3, cs], u[u.shape[0] - SUBLANES:, :]

    sub = tm // FFN_ROW_SPLIT
    for c in range(D_FF // FFN_CHUNK):
        gcol, vcol = c * FFN_CHUNK, D_FF + c * FFN_CHUNK
        g_tail, v_tail = tail_s[:, pl.ds(gcol, FFN_CHUNK)], tail_s[:, pl.ds(vcol, FFN_CHUNK)]
        for r in range(FFN_ROW_SPLIT):
            rows = pl.ds(r * sub, sub)
            gate, g_tail = conv(rows, gcol, g_tail)
            val, v_tail = conv(rows, vcol, v_tail)
            act_s[rows, pl.ds(c * FFN_CHUNK, FFN_CHUNK)] = (jax.nn.silu(gate) * val).astype(BF16)
        tail_s[:, pl.ds(gcol, FFN_CHUNK)] = g_tail
        tail_s[:, pl.ds(vcol, FFN_CHUNK)] = v_tail

    out = x + jnp.dot(act_s[...], wdn_ref[...], preferred_element_type=F32)
    if final_norm:
        out = _rms(out, fg_ref[...])
    o_ref[...] = out


def _ffn(res, ys, ws, transposed, layer, g, wup, cw, cb, wdn, fg, final_norm):
    s, d = res.shape
    tm = min(ROW_BLOCK, s)
    const = lambda a: pl.BlockSpec(a.shape, lambda i: (0, 0), pipeline_mode=pl.Buffered(1))
    stacked = lambda a: pl.BlockSpec((None,) + a.shape[1:], lambda i: (layer, 0, 0), pipeline_mode=pl.Buffered(1))
    y_specs = [pl.BlockSpec((y.shape[0], tm), lambda i: (0, i)) if tr else
               pl.BlockSpec((tm, y.shape[1]), lambda i: (i, 0)) for y, tr in zip(ys, transposed)]
    return pl.pallas_call(
        functools.partial(_ffn_body, transposed=transposed, final_norm=final_norm),
        grid=(s // tm,),
        in_specs=[pl.BlockSpec((tm, d), lambda i: (i, 0))] + y_specs + [const(w) for w in ws]
        + [stacked(g), stacked(wup), stacked(cw), stacked(cb), stacked(wdn), const(fg)],
        out_specs=pl.BlockSpec((tm, d), lambda i: (i, 0)),
        out_shape=jax.ShapeDtypeStruct((s, d), F32),
        scratch_shapes=[pltpu.VMEM((SUBLANES, 2 * D_FF), F32), pltpu.VMEM((tm, D_FF), BF16),
                        pltpu.VMEM((tm, d), BF16)],
        compiler_params=_params("arbitrary"),
        name="ffn_final" if final_norm else "ffn",
    )(res, *ys, *ws, g, wup, cw, cb, wdn, fg)


def _mlstm_body(q_ref, k_ref, v_ref, o_ref, gate_ref, gb_ref, ng_ref, y_ref, *state):
    c_s, m_s = state[:M_HEADS], state[M_HEADS:]

    @pl.when(pl.program_id(0) == 0)
    def _():
        for ref in state:
            ref[...] = jnp.zeros_like(ref)

    for ci in range(q_ref.shape[0] // M_CHUNK):
        rows = pl.ds(ci * M_CHUNK, M_CHUNK)
        _mlstm_chunk(q_ref.at[rows, :], k_ref.at[rows, :], v_ref.at[rows, :], o_ref.at[rows, :],
                     gate_ref.at[rows, :], gb_ref, ng_ref, y_ref.at[rows, :], c_s, m_s)


def _mlstm_chunk(q_ref, k_ref, v_ref, o_ref, gate_ref, gb_ref, ng_ref, y_ref, c_s, m_s):
    L = q_ref.shape[0]
    capped = GATE_CAP * jnp.tanh((gate_ref[...] + gb_ref[...]) / GATE_CAP)
    log_f = jax.nn.log_sigmoid(capped)
    row = lax.broadcasted_iota(jnp.int32, (L, LANES), 0)
    b_all = log_f
    d = 1
    while d < L:
        b_all = b_all + jnp.where(row >= d, pltpu.roll(b_all, d, axis=0), 0.0)
        d *= 2
    i_t = capped.T
    b_t = b_all.T

    tri = lax.broadcasted_iota(jnp.int32, (L, L), 1) <= lax.broadcasted_iota(jnp.int32, (L, L), 0)
    ones = jnp.ones((L, LANES), BF16)
    q_scale = M_QK ** -0.5
    heads = range(M_HEADS)
    qs = [q_ref[:, h * M_QK:(h + 1) * M_QK] for h in heads]
    ks = [k_ref[:, h * M_QK:(h + 1) * M_QK] for h in heads]
    vaugs = [jnp.concatenate([v_ref[:, h * M_V:(h + 1) * M_V], ones], axis=1) for h in heads]
    c_sts = [c_s[h][...] for h in heads]
    m_sts = [m_s[h][0:1, :] for h in heads]
    qks = [lax.dot_general(qs[h], ks[h], (((1,), (1,)), ((), ())), preferred_element_type=F32) for h in heads]
    qcs = [jnp.dot(qs[h], c_sts[h].astype(BF16), preferred_element_type=F32) for h in heads]

    lanes = lambda col: jnp.broadcast_to(col, (L, LANES))
    wide = lambda a, n: jnp.concatenate([a] * n, axis=1)

    gate = []
    for h in heads:
        b_c = lanes(b_all[:, M_HEADS + h:M_HEADS + h + 1])
        i_c = lanes(capped[:, h:h + 1])
        b_row = b_t[M_HEADS + h:M_HEADS + h + 1, :]
        i_row = i_t[h:h + 1, :]
        dmat = jnp.where(tri, wide(b_c, L // LANES) - b_row + i_row, -jnp.inf)
        inter = b_c + m_sts[h]
        m_t = jnp.maximum(inter, lanes(jnp.max(dmat, axis=-1, keepdims=True)))
        w = jnp.exp(dmat - wide(m_t, L // LANES)) * q_scale
        g = jnp.exp(inter - m_t) * q_scale
        b_last = b_c[L - 1:, :]
        w_end = b_last - b_c + i_c
        m_new = jnp.maximum(b_last + m_sts[h], jnp.max(w_end, axis=0, keepdims=True))
        g_end = jnp.exp(b_last + m_sts[h] - m_new)
        gate.append((w, g, m_t, m_new, g_end, jnp.exp(w_end - m_new)))

    n_aug = (M_V + LANES) // LANES
    nums = []
    for h in heads:
        w, g, _, m_new, g_end, w_s = gate[h]
        sc = (qks[h] * w).astype(BF16)
        nums.append(wide(g, n_aug) * qcs[h] + jnp.dot(sc, vaugs[h], preferred_element_type=F32))
        kw = (ks[h].astype(F32) * w_s).astype(BF16)
        c_s[h][...] = wide(g_end, n_aug) * c_sts[h] + lax.dot_general(
            kw, vaugs[h], (((0,), (0,)), ((), ())), preferred_element_type=F32)
        m_s[h][...] = jnp.broadcast_to(m_new, m_s[h].shape)

    for h in heads:
        m_t = gate[h][2]
        den = nums[h][:, M_V:]
        hh = nums[h][:, :M_V] / wide(jnp.maximum(jnp.abs(den), jnp.exp(-m_t)), M_V // LANES)
        msq = lanes(jnp.mean(hh * hh, axis=-1, keepdims=True))
        hn = hh * wide(lax.rsqrt(msq + EPS), M_V // LANES)
        vs = slice(h * M_V, (h + 1) * M_V)
        y_ref[:, vs] = (hn * ng_ref[:, vs] * jax.nn.sigmoid(o_ref[:, vs])).astype(y_ref.dtype)


def _mlstm(qkv, og, gate_bias, norm_g):
    s = qkv.shape[0]
    L = M_CHUNK * M_CHUNKS_PER_STEP
    assert s % L == 0
    qk = M_HEADS * M_QK
    vw = M_HEADS * M_V
    return pl.pallas_call(
        _mlstm_body,
        grid=(s // L,),
        in_specs=[
            pl.BlockSpec((L, qk), lambda i: (i, 0)),
            pl.BlockSpec((L, qk), lambda i: (i, 1)),
            pl.BlockSpec((L, vw), lambda i: (i, 2 * qk // vw)),
            pl.BlockSpec((L, vw), lambda i: (i, 0)),
            pl.BlockSpec((L, LANES), lambda i: (i, vw // LANES)),
            pl.BlockSpec((1, LANES), lambda i: (0, 0)),
            pl.BlockSpec((1, vw), lambda i: (0, 0)),
        ],
        out_specs=pl.BlockSpec((L, vw), lambda i: (i, 0)),
        out_shape=jax.ShapeDtypeStruct((s, vw), BF16),
        scratch_shapes=[pltpu.VMEM((M_QK, M_V + LANES), F32)] * M_HEADS + [pltpu.VMEM((SUBLANES, LANES), F32)] * M_HEADS,
        compiler_params=_params("arbitrary"),
        name="mlstm",
    )(qkv, qkv, qkv, og, og, gate_bias, norm_g)


def _rot_cols(w):
    half = QK_ROPE // 2
    return jnp.concatenate([-w[:, half:], w[:, :half]], axis=1)


def _place(w, start):
    return jnp.pad(w, ((0, 0), (start, HEAD_PAD - start - w.shape[1])))


def _block_diag(w):
    g, bi, bo = w.shape
    eye = jnp.eye(g, dtype=w.dtype)
    return (w[:, :, None, :] * eye[:, None, :, None]).reshape(g * bi, g * bo)


def kernel(x, positions, e_norm_g, e_w_in, e_lru_conv_w, e_lru_conv_b, e_lru_w_a, e_lru_b_a, e_lru_w_x,
           e_lru_b_x, e_lru_lambda, e_q_norm_g, e_w_qb, e_kv_norm_g, e_w_kvb, e_w_out, o_norm_g, o_w_in,
           o_b_igate, o_b_fgate, o_out_norm_g, o_w_out, f_norm_g, f_w_up, f_conv_w, f_conv_b, f_w_down,
           final_norm_g):
    bsz, s, d = x.shape
    assert bsz == 1 and d == D_MODEL
    h = x.reshape(s, d)
    pos = positions.reshape(1, s)
    depth = f_norm_g.shape[0]

    half = QK_ROPE // 2
    inv_freq = ROPE_THETA ** (-jnp.arange(half, dtype=F32) / half)
    freq = inv_freq.reshape(half, 1)
    f_w_up_b = f_w_up.astype(BF16)
    f_w_down_b = f_w_down.astype(BF16)

    for layer in range(depth):
        j = layer // 2
        if layer % 2 == 0:
            w_in = e_w_in[j]
            c_kr = 2 * LRU_WIDTH + Q_LORA + KV_LORA
            w_kr = w_in[:, c_kr:c_kr + QK_ROPE]
            w_kr_placed = jnp.concatenate([_place(w_kr, QK_NOPE), _place(_rot_cols(w_kr), QK_NOPE)], axis=1)
            z, zkr = _norm_matmul(h, e_norm_g[j], [w_in[:, :c_kr].astype(BF16), w_kr_placed.astype(BF16)],
                                  (F32, F32), "even_in")

            y_lru = _rglru(
                z, e_lru_conv_w[j], e_lru_conv_b[j][None, :],
                _block_diag(e_lru_w_a[j]).astype(BF16), e_lru_b_a[j].reshape(1, LRU_WIDTH),
                _block_diag(e_lru_w_x[j]).astype(BF16), e_lru_b_x[j].reshape(1, LRU_WIDTH),
                e_lru_lambda[j][None, :])

            wq = e_w_qb[j].reshape(Q_LORA, MLA_HEADS, QK_NOPE + QK_ROPE)
            wq_a = jnp.pad(wq, ((0, 0), (0, 0), (0, HEAD_PAD - QK_NOPE - QK_ROPE)))
            wq_pe = wq[:, :, QK_NOPE:]
            wq_rot = jnp.concatenate([-wq_pe[:, :, half:], wq_pe[:, :, :half]], axis=2)
            wq_b = jnp.pad(wq_rot, ((0, 0), (0, 0), (QK_NOPE, HEAD_PAD - QK_NOPE - QK_ROPE)))
            wkv = e_w_kvb[j].reshape(KV_LORA, MLA_HEADS, QK_NOPE + V_HEAD)
            wk = jnp.pad(wkv[:, :, :QK_NOPE], ((0, 0), (0, 0), (0, HEAD_PAD - QK_NOPE)))
            hw = MLA_HEADS * HEAD_PAD
            qt, k, vt = _mla_prep(
                z, zkr, pos, e_q_norm_g[j][None, :], e_kv_norm_g[j][None, :],
                wq_a.reshape(Q_LORA, hw).T.astype(BF16), wq_b.reshape(Q_LORA, hw).T.astype(BF16),
                wk.reshape(KV_LORA, hw).astype(BF16),
                wkv[:, :, QK_NOPE:].reshape(KV_LORA, MLA_HEADS * V_HEAD).T.astype(BF16), freq)
            y_mla_t = _attention(qt, k, vt)

            w_out = e_w_out[j].astype(BF16)
            mix = ([y_lru, y_mla_t], [w_out[:LRU_WIDTH], w_out[LRU_WIDTH:]], (False, True))
        else:
            w_in = o_w_in[j]
            w_og = jnp.pad(w_in[:, ODD_QKV_COLS:], ((0, 0), (0, ODD_COLS - w_in.shape[1])))
            qkv, og = _norm_matmul(h, o_norm_g[j], [w_in[:, :ODD_QKV_COLS].astype(BF16), w_og.astype(BF16)],
                                   (BF16, F32), "odd_in")
            gate_bias = jnp.pad(jnp.concatenate([o_b_igate[j], o_b_fgate[j]])[None, :],
                                ((0, 0), (0, LANES - 2 * M_HEADS)))
            y = _mlstm(qkv, og, gate_bias, o_out_norm_g[j][None, :])
            mix = ([y], [o_w_out[j].astype(BF16)], (False,))

        h = _ffn(h, *mix, layer, f_norm_g[:, None, :], f_w_up_b, f_conv_w, f_conv_b[:, None, :], f_w_down_b,
                 final_norm_g[None, :], final_norm=(layer == depth - 1))
    return h.reshape(bsz, s, d)
```
